```python
import jax
import jax.numpy as jnp
from jax import lax
import numpy as np

D_MODEL = 1024
BATCH = 16
SEQ = 2048
DEPTH = 2

CTX_LEN = 256
GRID_W = 64
HEAD_DIM = 128
N_HEADS = 8
N_KV_HEADS = 2
WINDOW = 128
BLOCK = 128
ROPE_THETA = 10000.0
LRU_W = 1024
LRU_BLOCKS = 8
LRU_BW = LRU_W // LRU_BLOCKS
LRU_C = 8.0
CONV_W = 4
CONV_LEFT = 2
N_EXPERTS = 16
EC_FACTOR = 2
D_EXPERT = 2048
N_MOD = 6
EPS = 1e-6
Q_W = N_HEADS * HEAD_DIM
KV_W = N_KV_HEADS * HEAD_DIM
IN_SPLITS = (Q_W, KV_W, KV_W, LRU_W, LRU_W, D_MODEL, D_MODEL)
IN_W = Q_W + 2 * KV_W + 2 * LRU_W + 2 * D_MODEL

kernel_name = 'hybrid_dit_swa_rglru_ecmoe'


def rmsnorm(x, g):
    xf = x.astype(jnp.float32)
    y = xf * lax.rsqrt(jnp.mean(xf * xf, axis=-1, keepdims=True) + EPS)
    return (y * g.astype(jnp.float32)).astype(x.dtype)


def modulate(h, shift, scale):
    return h * (1 + scale) + shift


def split_in(proj):
    bounds = np.cumsum(IN_SPLITS)[:-1].tolist()
    return jnp.split(proj, bounds, axis=-1)


def to_heads(t, n):
    return t.reshape(t.shape[:-1] + (n, HEAD_DIM))


def rope_1d(v, pos):
    half = v.shape[-1] // 2
    freqs = ROPE_THETA ** (-jnp.arange(half, dtype=jnp.float32) / half)
    ang = pos.astype(jnp.float32)[:, None] * freqs[None, :]
    cos = jnp.cos(ang)[:, None, :].astype(v.dtype)
    sin = jnp.sin(ang)[:, None, :].astype(v.dtype)
    v1, v2 = v[..., :half], v[..., half:]
    return jnp.concatenate([v1 * cos - v2 * sin, v1 * sin + v2 * cos], axis=-1)


def axial_rope(v, rows, cols):
    h = v.shape[-1] // 2
    return jnp.concatenate([rope_1d(v[..., :h], rows), rope_1d(v[..., h:], cols)], axis=-1)


def sink_softmax(scores, sink):
    m = sink
    for s in scores:
        m = jnp.maximum(m, jnp.max(s, axis=-1))
    ps = [jnp.exp(s - m[..., None]) for s in scores]
    denom = jnp.exp(sink - m)
    for p in ps:
        denom = denom + jnp.sum(p, axis=-1)
    return [p / denom[..., None] for p in ps]


def _band(t, nb):
    B, S, KV, d = t.shape
    tp = jnp.pad(t, ((0, 0), (BLOCK, BLOCK), (0, 0), (0, 0))).reshape(B, nb + 2, BLOCK, KV, d)
    return jnp.concatenate([tp[:, :-2], tp[:, 1:-1], tp[:, 2:]], axis=2)


def banded_attention(q, k, v, k_ctx, v_ctx, sink):
    B, S, H, d = q.shape
    G = H // N_KV_HEADS
    nb = S // BLOCK
    scale = HEAD_DIM ** -0.5
    qb = q.reshape(B, nb, BLOCK, N_KV_HEADS, G, d)
    kb, vb = _band(k, nb), _band(v, nb)
    qpos = jnp.arange(S).reshape(nb, BLOCK)
    kpos = (jnp.arange(nb)[:, None] - 1) * BLOCK + jnp.arange(3 * BLOCK)[None, :]
    kp = kpos[:, None, :]
    valid = (kp >= 0) & (kp < S) & (jnp.abs(qpos[:, :, None] - kp) <= WINDOW)
    s_band = jnp.einsum('bnqkgd,bnskd->bnkgqs', qb, kb).astype(jnp.float32) * scale
    s_band = jnp.where(valid[None, :, None, None], s_band, -jnp.inf)
    s_ctx = jnp.einsum('bnqkgd,blkd->bnkgql', qb, k_ctx).astype(jnp.float32) * scale
    p_band, p_ctx = sink_softmax([s_band, s_ctx], sink.reshape(N_KV_HEADS, G, 1).astype(jnp.float32))
    o = (jnp.einsum('bnkgqs,bnskd->bnqkgd', p_band.astype(v.dtype), vb)
         + jnp.einsum('bnkgql,blkd->bnqkgd', p_ctx.astype(v.dtype), v_ctx))
    return o.reshape(B, S, H * d)


def context_attention(q, k, v, sink):
    B, L, H, d = q.shape
    G = H // N_KV_HEADS
    qg = q.reshape(B, L, N_KV_HEADS, G, d)
    s = jnp.einsum('blkgd,bmkd->bkglm', qg, k).astype(jnp.float32) * (HEAD_DIM ** -0.5)
    (p,) = sink_softmax([s], sink.reshape(N_KV_HEADS, G, 1).astype(jnp.float32))
    o = jnp.einsum('bkglm,bmkd->blkgd', p.astype(v.dtype), v)
    return o.reshape(B, L, H * d)


def short_conv(u, w, b):
    T = u.shape[1]
    up = jnp.pad(u, ((0, 0), (CONV_LEFT, CONV_W - 1 - CONV_LEFT), (0, 0)))
    out = b
    for kk in range(CONV_W):
        out = out + w[kk] * up[:, kk:kk + T]
    return out


def _combine(e1, e2):
    a1, b1 = e1
    a2, b2 = e2
    return a1 * a2, a2 * b1 + b2


def linear_scan(a, b, h0, reverse):
    edge = -1 if reverse else 0
    b = b.at[:, edge].add(a[:, edge] * h0)
    _, h = lax.associative_scan(_combine, (a, b), reverse=reverse, axis=1)
    return h


def rg_lru(u, wa, ba, wx, bx, lam, h0, reverse):
    B, T, W = u.shape
    ub = u.reshape(B, T, LRU_BLOCKS, LRU_BW)
    gate_r = jax.nn.sigmoid(jnp.einsum('btni,nij->btnj', ub, wa).reshape(B, T, W) + ba)
    gate_i = jax.nn.sigmoid(jnp.einsum('btni,nij->btnj', ub, wx).reshape(B, T, W) + bx)
    log_a = (-LRU_C * gate_r.astype(jnp.float32)) * jax.nn.softplus(-lam.astype(jnp.float32))
    a = jnp.exp(log_a)
    b = jnp.sqrt(-jnp.expm1(2.0 * log_a)) * (gate_i * u).astype(jnp.float32)
    return linear_scan(a, b, h0, reverse)


def recurrent_mixer(u_ctx, u_lat, conv_w, conv_b, lru_wa, lru_ba, lru_wx, lru_bx, lru_lambda, need_ctx_out):
    uc = short_conv(u_ctx, conv_w, conv_b)
    ul = short_conv(u_lat, conv_w, conv_b)
    h0 = jnp.zeros((u_ctx.shape[0], LRU_W), jnp.float32)
    hc_f = rg_lru(uc, lru_wa[0], lru_ba[0], lru_wx[0], lru_bx[0], lru_lambda[0], h0, False)
    hc_b = rg_lru(uc, lru_wa[1], lru_ba[1], lru_wx[1], lru_bx[1], lru_lambda[1], h0, True)
    hl_f = rg_lru(ul, lru_wa[0], lru_ba[0], lru_wx[0], lru_bx[0], lru_lambda[0], hc_f[:, -1], False)
    hl_b = rg_lru(ul, lru_wa[1], lru_ba[1], lru_wx[1], lru_bx[1], lru_lambda[1], hc_b[:, 0], True)
    y_lat = (hl_f + hl_b).astype(u_lat.dtype)
    y_ctx = (hc_f + hc_b).astype(u_ctx.dtype) if need_ctx_out else None
    return y_lat, y_ctx


def merge_branches(att, rec, z, m_att, m_rec, w_attn_br, w_rec_br, w_out):
    att_d = att @ w_attn_br
    rec_d = (rec * jax.nn.gelu(z)) @ w_rec_br
    return (jax.nn.sigmoid(m_att) * att_d + jax.nn.sigmoid(m_rec) * rec_d) @ w_out


def expert_choice_moe(h, w_router, w_gate, w_up, w_down):
    B, n, D = h.shape
    cap = EC_FACTOR * n // N_EXPERTS
    aff = jax.nn.softmax((h @ w_router).astype(jnp.float32), axis=-1)
    g, idx = lax.top_k(jnp.swapaxes(aff, 1, 2), cap)
    xe = jax.vmap(lambda hb, ib: hb[ib])(h, idx)
    hid = jax.nn.silu(jnp.einsum('becd,edf->becf', xe, w_gate)) * jnp.einsum('becd,edf->becf', xe, w_up)
    ye = jnp.einsum('becf,efd->becd', hid, w_down) * g[..., None].astype(h.dtype)
    return jax.vmap(lambda ib, yb: jnp.zeros((n, D), yb.dtype).at[ib.reshape(-1)].add(yb.reshape(-1, D)))(idx, ye)


def hybrid_layer(x, ctx, mod_lat, mod_ctx, rows, cols, norm_mix_g, w_in, attn_sink, conv_w, conv_b,
                 lru_wa, lru_ba, lru_wx, lru_bx, lru_lambda, w_attn_br, w_rec_br, w_out,
                 norm_ffn_g, w_router, w_gate, w_up, w_down, last):
    sh1, sc1, g1, sh2, sc2, g2 = jnp.split(mod_lat, N_MOD, axis=-1)
    csh1, csc1, cg1, csh2, csc2, cg2 = jnp.split(mod_ctx, N_MOD, axis=-1)
    hl = modulate(rmsnorm(x, norm_mix_g), sh1, sc1)
    hc = modulate(rmsnorm(ctx, norm_mix_g), csh1, csc1)
    ql, kl, vl, ul, zl, mal, mrl = split_in(hl @ w_in)
    qc, kc, vc, uc, zc, mac, mrc = split_in(hc @ w_in)
    kc_h, vc_h = to_heads(kc, N_KV_HEADS), to_heads(vc, N_KV_HEADS)
    att_lat = banded_attention(axial_rope(to_heads(ql, N_HEADS), rows, cols),
                               axial_rope(to_heads(kl, N_KV_HEADS), rows, cols),
                               to_heads(vl, N_KV_HEADS), kc_h, vc_h, attn_sink)
    rec_lat, rec_ctx = recurrent_mixer(uc, ul, conv_w, conv_b, lru_wa, lru_ba, lru_wx, lru_bx, lru_lambda, not last)
    x = x + g1 * merge_branches(att_lat, rec_lat, zl, mal, mrl, w_attn_br, w_rec_br, w_out)
    x = x + g2 * expert_choice_moe(modulate(rmsnorm(x, norm_ffn_g), sh2, sc2), w_router, w_gate, w_up, w_down)
    if not last:
        att_ctx = context_attention(to_heads(qc, N_HEADS), kc_h, vc_h, attn_sink)
        ctx = ctx + cg1 * merge_branches(att_ctx, rec_ctx, zc, mac, mrc, w_attn_br, w_rec_br, w_out)
        ctx = ctx + cg2 * expert_choice_moe(modulate(rmsnorm(ctx, norm_ffn_g), csh2, csc2), w_router, w_gate, w_up, w_down)
    return x, ctx


def setup_inputs(seed: int = 0) -> dict:
    key = jax.random.key(seed)
    ks = jax.random.split(key, 26)
    f32 = jnp.float32
    D = D_MODEL

    def nrm(k, shape, scale):
        return jax.random.normal(k, shape, f32) * scale

    a0 = jax.random.uniform(ks[12], (DEPTH, 2, LRU_W), f32, 0.9, 0.999) ** (1.0 / LRU_C)
    return {
        'x': nrm(ks[0], (BATCH, SEQ, D), 1.0),
        'c': nrm(ks[1], (BATCH, D), 1.0),
        'ctx': nrm(ks[2], (BATCH, CTX_LEN, D), 1.0),
        'c_ctx': nrm(ks[3], (D,), 1.0),
        'ada_w': nrm(ks[4], (DEPTH, D, N_MOD * D), 0.5 * D ** -0.5),
        'ada_b': nrm(ks[5], (DEPTH, N_MOD * D), 0.02),
        'norm_mix_g': 1.0 + nrm(ks[6], (DEPTH, D), 0.02),
        'w_in': nrm(ks[7], (DEPTH, D, IN_W), D ** -0.5),
        'attn_sink': nrm(ks[8], (DEPTH, N_HEADS), 0.5),
        'conv_w': nrm(ks[9], (DEPTH, CONV_W, LRU_W), CONV_W ** -0.5),
        'conv_b': nrm(ks[10], (DEPTH, LRU_W), 0.02),
        'lru_wa': nrm(ks[11], (DEPTH, 2, LRU_BLOCKS, LRU_BW, LRU_BW), LRU_BW ** -0.5),
        'lru_ba': nrm(ks[13], (DEPTH, 2, LRU_W), 0.02),
        'lru_wx': nrm(ks[14], (DEPTH, 2, LRU_BLOCKS, LRU_BW, LRU_BW), LRU_BW ** -0.5),
        'lru_bx': nrm(ks[15], (DEPTH, 2, LRU_W), 0.02),
        'lru_lambda': jnp.log(a0) - jnp.log1p(-a0),
        'w_attn_br': nrm(ks[16], (DEPTH, Q_W, D), Q_W ** -0.5),
        'w_rec_br': nrm(ks[17], (DEPTH, LRU_W, D), LRU_W ** -0.5),
        'w_out': nrm(ks[18], (DEPTH, D, D), D ** -0.5),
        'norm_ffn_g': 1.0 + nrm(ks[19], (DEPTH, D), 0.02),
        'w_router': nrm(ks[20], (DEPTH, D, N_EXPERTS), D ** -0.5),
        'w_gate': nrm(ks[21], (DEPTH, N_EXPERTS, D, D_EXPERT), D ** -0.5),
        'w_up': nrm(ks[22], (DEPTH, N_EXPERTS, D, D_EXPERT), D ** -0.5),
        'w_down': nrm(ks[23], (DEPTH, N_EXPERTS, D_EXPERT, D), D_EXPERT ** -0.5),
        'final_norm_g': 1.0 + nrm(ks[24], (D,), 0.02),
    }


def reference(x, c, ctx, c_ctx, ada_w, ada_b, norm_mix_g, w_in, attn_sink, conv_w, conv_b,
              lru_wa, lru_ba, lru_wx, lru_bx, lru_lambda, w_attn_br, w_rec_br, w_out,
              norm_ffn_g, w_router, w_gate, w_up, w_down, final_norm_g):
    S = x.shape[1]
    ROWS = S // GRID_W
    rows = jnp.repeat(jnp.arange(ROWS, dtype=jnp.int32), GRID_W)
    cols = jnp.tile(jnp.arange(GRID_W, dtype=jnp.int32), ROWS)
    c_act = jax.nn.silu(c)
    cctx_act = jax.nn.silu(c_ctx)
    for l in range(DEPTH):
        mod_lat = (c_act @ ada_w[l] + ada_b[l])[:, None, :]
        mod_ctx = cctx_act @ ada_w[l] + ada_b[l]
        x, ctx = hybrid_layer(x, ctx, mod_lat, mod_ctx, rows, cols, norm_mix_g[l], w_in[l], attn_sink[l],
                              conv_w[l], conv_b[l], lru_wa[l], lru_ba[l], lru_wx[l], lru_bx[l], lru_lambda[l],
                              w_attn_br[l], w_rec_br[l], w_out[l], norm_ffn_g[l], w_router[l],
                              w_gate[l], w_up[l], w_down[l], l == DEPTH - 1)
    return rmsnorm(x, final_norm_g)
```

```python
import functools
import math

import jax
import jax.numpy as jnp
from jax import lax
from jax.experimental import pallas as pl
from jax.experimental.pallas import tpu as pltpu

F32 = jnp.float32
BF16 = jnp.bfloat16
I32 = jnp.int32

HEAD_DIM = 128
N_HEADS = 8
N_KV_HEADS = 2
GROUP = N_HEADS // N_KV_HEADS
ATT_BLOCK = 128
GRID_W = 64
ROPE_THETA = 10000.0
LRU_BLOCKS = 8
LRU_C = 8.0
CONV_W = 4
CONV_LEFT = 2
N_EXPERTS = 16
EC_FACTOR = 2
N_MOD = 6
EPS = 1e-6
NEG_BIG = -1e30

LANES = 128
SUBLANES = 8
SCAN_SEGMENTS = 64
VMEM_LIMIT = 56 * 1024 * 1024


def _sigmoid(x):
    return 0.5 * jnp.tanh(0.5 * x) + 0.5


def _gelu_tanh(x):
    c = math.sqrt(2.0 / math.pi)
    return 0.5 * x * (1.0 + jnp.tanh(c * (x + 0.044715 * (x * x * x))))


def _rmsnorm(x, g):
    ms = jnp.mean(x * x, axis=-1, keepdims=True)
    return x * lax.rsqrt(ms + EPS) * g


def _params(sem):
    return pltpu.CompilerParams(dimension_semantics=sem, vmem_limit_bytes=VMEM_LIMIT)


def _ada_body(c_ref, w_ref, b_ref, o_ref):
    c = c_ref[...]
    act = c * _sigmoid(c)
    o_ref[...] = jnp.dot(act, w_ref[...], preferred_element_type=F32,
                         precision=lax.Precision.HIGHEST) + b_ref[...]


def _ada_call(cc, ada_w, ada_b):
    depth, d, n6 = ada_w.shape
    rows = cc.shape[0]
    tn = 1024
    return pl.pallas_call(
        _ada_body,
        grid=(depth, n6 // tn),
        in_specs=[pl.BlockSpec((rows, d), lambda l, j: (0, 0)),
                  pl.BlockSpec((None, d, tn), lambda l, j: (l, 0, j)),
                  pl.BlockSpec((None, 1, tn), lambda l, j: (l, 0, j))],
        out_specs=pl.BlockSpec((None, rows, tn), lambda l, j: (l, 0, j)),
        out_shape=jax.ShapeDtypeStruct((depth, rows, n6), F32),
        compiler_params=_params(("parallel", "parallel")),
        name="ada_mod",
    )(cc, ada_w, ada_b.reshape(depth, 1, n6))


def _rope(xh, cos, ssin, hi):
    sw = jnp.where(hi, pltpu.roll(xh, 32, 1), pltpu.roll(xh, LANES - 32, 1))
    return xh * cos + sw * ssin


def _inproj_body(x_ref, mod_ref, g_ref, w_ref, cos_ref, sin_ref,
                 q_ref, k_ref, v_ref, u_ref, z_ref, ma_ref, mr_ref, *, d):
    x = x_ref[...]
    h = _rmsnorm(x, g_ref[...]) * (1.0 + mod_ref[1:2, :]) + mod_ref[0:1, :]
    hb = h.astype(BF16)
    cos = cos_ref[...]
    ssin = sin_ref[...]
    hi = (lax.broadcasted_iota(I32, cos.shape, 1) & 32) != 0
    qw = N_HEADS * HEAD_DIM
    kw = N_KV_HEADS * HEAD_DIM

    def proj(lo, width):
        return jnp.dot(hb, w_ref[:, lo:lo + width], preferred_element_type=F32)

    for hh in range(N_HEADS):
        qh = proj(hh * HEAD_DIM, HEAD_DIM)
        q_ref[:, hh * HEAD_DIM:(hh + 1) * HEAD_DIM] = _rope(qh, cos, ssin, hi).astype(BF16)
    for hh in range(N_KV_HEADS):
        kh = proj(qw + hh * HEAD_DIM, HEAD_DIM)
        k_ref[:, hh * HEAD_DIM:(hh + 1) * HEAD_DIM] = _rope(kh, cos, ssin, hi).astype(BF16)
    off = qw + kw
    v_ref[...] = proj(off, kw).astype(BF16)
    off += kw
    u_ref[...] = proj(off, d)
    off += d
    z_ref[...] = proj(off, d).astype(BF16)
    off += d
    ma_ref[...] = proj(off, d).astype(BF16)
    off += d
    mr_ref[...] = proj(off, d).astype(BF16)


def _inproj_call(xc, modcat, g, w_in_bf, cos_t, sin_t, *, n_ctx, tt):
    b, t, d = xc.shape
    in_w = w_in_bf.shape[1]
    nct = n_ctx // tt
    qw = N_HEADS * HEAD_DIM
    kw = N_KV_HEADS * HEAD_DIM
    row = lambda bi, ti: (bi, ti, 0)
    outs = [jax.ShapeDtypeStruct((b, t, qw), BF16), jax.ShapeDtypeStruct((b, t, kw), BF16),
            jax.ShapeDtypeStruct((b, t, kw), BF16), jax.ShapeDtypeStruct((b, t, d), F32),
            jax.ShapeDtypeStruct((b, t, d), BF16), jax.ShapeDtypeStruct((b, t, d), BF16),
            jax.ShapeDtypeStruct((b, t, d), BF16)]
    return pl.pallas_call(
        functools.partial(_inproj_body, d=d),
        grid=(b, t // tt),
        in_specs=[pl.BlockSpec((None, tt, d), row),
                  pl.BlockSpec((None, None, N_MOD, d), lambda bi, ti: (bi, jnp.where(ti >= nct, 1, 0), 0, 0)),
                  pl.BlockSpec((1, d), lambda bi, ti: (0, 0)),
                  pl.BlockSpec((d, in_w), lambda bi, ti: (0, 0)),
                  pl.BlockSpec((tt, HEAD_DIM), lambda bi, ti: (ti, 0)),
                  pl.BlockSpec((tt, HEAD_DIM), lambda bi, ti: (ti, 0))],
        out_specs=[pl.BlockSpec((None, tt, s.shape[2]), row) for s in outs],
        out_shape=outs,
        compiler_params=_params(("parallel", "parallel")),
        name="in_proj",
    )(xc, modcat, g, w_in_bf, cos_t, sin_t)


def _attn_body(q_ref, k_ref, v_ref, sink_ref, o_ref, *, n_ctx, t_all, blk_off):
    kh = pl.program_id(1)
    n = pl.program_id(2) + blk_off
    nc = n_ctx // ATT_BLOCK
    band = 3 * ATT_BLOCK
    q = q_ref[...]
    qs = jnp.concatenate([q[:, g * HEAD_DIM:(g + 1) * HEAD_DIM] for g in range(GROUP)], axis=0)
    rows = GROUP * ATT_BLOCK

    start = jnp.clip(n_ctx + (n - nc - 1) * ATT_BLOCK, 0, t_all - band)
    start = pl.multiple_of(start, ATT_BLOCK)
    kb = k_ref[pl.ds(start, band), :]
    vb = v_ref[pl.ds(start, band), :]
    kc = k_ref[0:n_ctx, :]
    vc = v_ref[0:n_ctx, :]

    scale = HEAD_DIM ** -0.5
    dn = (((1,), (1,)), ((), ()))
    s_band = lax.dot_general(qs, kb, dn, preferred_element_type=F32) * scale
    s_ctx = lax.dot_general(qs, kc, dn, preferred_element_type=F32) * scale

    qpos = (n - nc) * ATT_BLOCK + (lax.broadcasted_iota(I32, (rows, band), 0) & (ATT_BLOCK - 1))
    kpos = start - n_ctx + lax.broadcasted_iota(I32, (rows, band), 1)
    valid = (kpos >= 0) & (jnp.abs(qpos - kpos) <= ATT_BLOCK) & (n >= nc)
    s_band = jnp.where(valid, s_band, NEG_BIG)

    sink = jnp.concatenate(
        [jnp.broadcast_to(sink_ref[pl.ds(kh * GROUP + g, 1), 0:1], (ATT_BLOCK, 1)) for g in range(GROUP)], axis=0)
    m = jnp.maximum(sink, jnp.maximum(jnp.max(s_band, axis=-1, keepdims=True),
                                      jnp.max(s_ctx, axis=-1, keepdims=True)))
    p_band = jnp.exp(s_band - m)
    p_ctx = jnp.exp(s_ctx - m)
    denom = jnp.exp(sink - m) + jnp.sum(p_band, axis=-1, keepdims=True) + jnp.sum(p_ctx, axis=-1, keepdims=True)
    o = (jnp.dot(p_band.astype(BF16), vb, preferred_element_type=F32)
         + jnp.dot(p_ctx.astype(BF16), vc, preferred_element_type=F32)) / denom
    o_ref[...] = jnp.concatenate(
        [o[g * ATT_BLOCK:(g + 1) * ATT_BLOCK, :] for g in range(GROUP)], axis=1).astype(BF16)


def _attn_call(q, k, v, sink_rows, *, n_ctx, with_ctx):
    b, t, qw = q.shape
    blk_off = 0 if with_ctx else n_ctx // ATT_BLOCK
    nblk = t // ATT_BLOCK - blk_off
    gw = GROUP * HEAD_DIM
    return pl.pallas_call(
        functools.partial(_attn_body, n_ctx=n_ctx, t_all=t, blk_off=blk_off),
        grid=(b, N_KV_HEADS, nblk),
        in_specs=[pl.BlockSpec((None, ATT_BLOCK, gw), lambda bi, hi, ni: (bi, ni + blk_off, hi)),
                  pl.BlockSpec((None, t, HEAD_DIM), lambda bi, hi, ni: (bi, 0, hi)),
                  pl.BlockSpec((None, t, HEAD_DIM), lambda bi, hi, ni: (bi, 0, hi)),
                  pl.BlockSpec((N_HEADS, LANES), lambda bi, hi, ni: (0, 0))],
        out_specs=pl.BlockSpec((None, ATT_BLOCK, gw), lambda bi, hi, ni: (bi, ni + blk_off, hi)),
        out_shape=jax.ShapeDtypeStruct((b, t, qw), BF16),
        compiler_params=_params(("parallel", "parallel", "arbitrary")),
        name="win_attn",
    )(q, k, v, sink_rows)


def _lru_body(u_ref, cw_ref, cb_ref, w_ref, b_ref, lam_ref, y_ref,
              upad, a_f, b_f, a_b, b_b, h_f, h_b, *, n_ctx, n_lat, chunk):
    t_all = n_ctx + n_lat
    pad = SUBLANES
    zeros_pad = jnp.zeros((pad, LANES), F32)
    upad[0:pad, :] = zeros_pad
    upad[pad:pad + n_ctx, :] = u_ref[0:n_ctx, :]
    upad[pad + n_ctx:2 * pad + n_ctx, :] = zeros_pad
    upad[2 * pad + n_ctx:2 * pad + t_all, :] = u_ref[n_ctx:t_all, :]
    upad[2 * pad + t_all:3 * pad + t_all, :] = zeros_pad

    lam = lam_ref[...]
    nl = -lam
    sp = jnp.maximum(nl, 0.0) + jnp.log(1.0 + jnp.exp(-jnp.abs(nl)))
    cw = cw_ref[...]
    cb = cb_ref[...]
    wcat = w_ref[...]
    bcat = b_ref[...]

    def gate_chunk(ci, carry):
        r0 = pl.multiple_of(ci * chunk, chunk)
        in_lat = r0 >= n_ctx
        p0 = r0 + pad + jnp.where(in_lat, pad, 0)
        win = upad[pl.ds(pl.multiple_of(p0 - pad, pad), chunk + 2 * pad), :]
        uc = cb
        for kk in range(CONV_W):
            lo = pad + kk - CONV_LEFT
            uc = uc + cw[kk:kk + 1, :] * win[lo:lo + chunk, :]
        gates = jnp.dot(uc.astype(BF16), wcat, preferred_element_type=F32) + bcat
        dest_b = pl.multiple_of(jnp.where(in_lat, r0 - n_ctx, r0 + n_lat), chunk)
        for dr, (a_ref, bb_ref, dest) in enumerate(((a_f, b_f, r0), (a_b, b_b, dest_b))):
            g_r = _sigmoid(gates[:, (2 * dr) * LANES:(2 * dr + 1) * LANES])
            g_i = _sigmoid(gates[:, (2 * dr + 1) * LANES:(2 * dr + 2) * LANES])
            a = jnp.exp((-LRU_C) * g_r * sp[dr:dr + 1, :])
            bb = jnp.sqrt(jnp.maximum(1.0 - a * a, 0.0)) * (g_i * uc)
            a_ref[pl.ds(dest, chunk), :] = a
            bb_ref[pl.ds(dest, chunk), :] = bb
        return carry

    lax.fori_loop(0, t_all // chunk, gate_chunk, 0)

    nseg = SCAN_SEGMENTS
    seg = t_all // nseg

    def rows(i):
        return pl.ds(i, nseg, stride=seg)

    ones = jnp.ones((nseg, LANES), F32)
    zeros = jnp.zeros((nseg, LANES), F32)

    def pass1(i, carry):
        hf, pf, hb, pb = carry
        j = seg - 1 - i
        af = a_f[rows(i), :]
        hf = af * hf + b_f[rows(i), :]
        pf = af * pf
        h_f[rows(i), :] = hf
        ab = a_b[rows(j), :]
        hb = ab * hb + b_b[rows(j), :]
        pb = ab * pb
        h_b[rows(j), :] = hb
        return hf, pf, hb, pb

    hf_fin, pf_tot, hb_fin, pb_tot = lax.fori_loop(0, seg, pass1, (zeros, ones, zeros, ones))

    hin = jnp.zeros((1, LANES), F32)
    hin_f = []
    for s in range(nseg):
        hin_f.append(hin)
        hin = hf_fin[s:s + 1, :] + pf_tot[s:s + 1, :] * hin
    hin = jnp.zeros((1, LANES), F32)
    hin_b = [None] * nseg
    for s in range(nseg - 1, -1, -1):
        hin_b[s] = hin
        hin = hb_fin[s:s + 1, :] + pb_tot[s:s + 1, :] * hin
    hin_f = jnp.concatenate(hin_f, axis=0)
    hin_b = jnp.concatenate(hin_b, axis=0)

    def pass2(i, carry):
        pf, pb = carry
        j = seg - 1 - i
        pf = a_f[rows(i), :] * pf
        h_f[rows(i), :] = h_f[rows(i), :] + pf * hin_f
        pb = a_b[rows(j), :] * pb
        h_b[rows(j), :] = h_b[rows(j), :] + pb * hin_b
        return pf, pb

    lax.fori_loop(0, seg, pass2, (ones, ones))

    y_ref[0:n_ctx, :] = (h_f[0:n_ctx, :] + h_b[n_lat:t_all, :]).astype(BF16)
    y_ref[n_ctx:t_all, :] = (h_f[n_ctx:t_all, :] + h_b[0:n_lat, :]).astype(BF16)


def _lru_call(u, conv_w, conv_b, wcat, bcat, lam, *, n_ctx):
    b, t, w = u.shape
    n_lat = t - n_ctx
    chunk = math.gcd(math.gcd(n_ctx, n_lat), 256)
    assert t % SCAN_SEGMENTS == 0 and w == LRU_BLOCKS * LANES
    scr = [pltpu.VMEM((t + 3 * SUBLANES, LANES), F32)] + [pltpu.VMEM((t, LANES), F32) for _ in range(6)]
    return pl.pallas_call(
        functools.partial(_lru_body, n_ctx=n_ctx, n_lat=n_lat, chunk=chunk),
        grid=(b, LRU_BLOCKS),
        in_specs=[pl.BlockSpec((None, t, LANES), lambda bi, ci: (bi, 0, ci)),
                  pl.BlockSpec((CONV_W, LANES), lambda bi, ci: (0, ci)),
                  pl.BlockSpec((1, LANES), lambda bi, ci: (0, ci)),
                  pl.BlockSpec((None, LANES, 4 * LANES), lambda bi, ci: (ci, 0, 0)),
                  pl.BlockSpec((None, 1, 4 * LANES), lambda bi, ci: (ci, 0, 0)),
                  pl.BlockSpec((2, LANES), lambda bi, ci: (0, ci))],
        out_specs=pl.BlockSpec((None, t, LANES), lambda bi, ci: (bi, 0, ci)),
        out_shape=jax.ShapeDtypeStruct((b, t, w), BF16),
        scratch_shapes=scr,
        compiler_params=_params(("parallel", "parallel")),
        name="rg_lru",
    )(u, conv_w, conv_b, wcat, bcat, lam)


def _merge_body(x_ref, att_ref, y_ref, z_ref, ma_ref, mr_ref, mod_ref, g_ref,
                wab_ref, wrb_ref, wo_ref, wrt_ref, xo_ref, h2_ref, aff_ref):
    att_d = jnp.dot(att_ref[...], wab_ref[...], preferred_element_type=F32)
    rec_in = (y_ref[...].astype(F32) * _gelu_tanh(z_ref[...].astype(F32))).astype(BF16)
    rec_d = jnp.dot(rec_in, wrb_ref[...], preferred_element_type=F32)
    mix = _sigmoid(ma_ref[...].astype(F32)) * att_d + _sigmoid(mr_ref[...].astype(F32)) * rec_d
    out = jnp.dot(mix.astype(BF16), wo_ref[...], preferred_element_type=F32)
    xn = x_ref[...] + mod_ref[2:3, :] * out
    xo_ref[...] = xn
    h2 = _rmsnorm(xn, g_ref[...]) * (1.0 + mod_ref[4:5, :]) + mod_ref[3:4, :]
    h2_ref[...] = h2.astype(BF16)
    logits = lax.dot_general(wrt_ref[...], h2, (((1,), (1,)), ((), ())),
                             preferred_element_type=F32, precision=lax.Precision.HIGHEST)
    mx = jnp.max(logits, axis=0, keepdims=True)
    ex = jnp.exp(logits - mx)
    aff_ref[...] = ex / jnp.sum(ex, axis=0, keepdims=True)


def _merge_call(xc, att, y, z, ma, mr, modcat, g, wab, wrb, wo, wrt, *, n_ctx, tt, with_ctx):
    b, t, d = xc.shape
    nct = n_ctx // tt
    off = 0 if with_ctx else nct
    t_out = t - off * tt
    row = lambda bi, ti: (bi, ti + off, 0)
    orow = lambda bi, ti: (bi, ti, 0)
    full = lambda bi, ti: (0, 0)
    e = wrt.shape[0]
    return pl.pallas_call(
        _merge_body,
        grid=(b, t // tt - off),
        in_specs=[pl.BlockSpec((None, tt, d), row)] * 6 + [
            pl.BlockSpec((None, None, N_MOD, d), lambda bi, ti: (bi, jnp.where(ti + off >= nct, 1, 0), 0, 0)),
            pl.BlockSpec((1, d), full),
            pl.BlockSpec(wab.shape, full), pl.BlockSpec(wrb.shape, full), pl.BlockSpec(wo.shape, full),
            pl.BlockSpec(wrt.shape, full)],
        out_specs=[pl.BlockSpec((None, tt, d), orow), pl.BlockSpec((None, tt, d), orow),
                   pl.BlockSpec((None, e, tt), lambda bi, ti: (bi, 0, ti))],
        out_shape=[jax.ShapeDtypeStruct((b, t_out, d), F32), jax.ShapeDtypeStruct((b, t_out, d), BF16),
                   jax.ShapeDtypeStruct((b, e, t_out), F32)],
        compiler_params=_params(("parallel", "parallel")),
        name="merge_router",
    )(xc, att, y, z, ma, mr, modcat, g, wab, wrb, wo, wrt)


def _prefix_excl(mask, tri):
    rows, length = mask.shape
    run = jnp.zeros((rows, 1), F32)
    parts = []
    for kb in range(length // LANES):
        blk = mask[:, kb * LANES:(kb + 1) * LANES]
        loc = jnp.dot(blk.astype(BF16), tri, preferred_element_type=F32)
        parts.append(loc + run)
        run = run + jnp.sum(blk, axis=1, keepdims=True)
    return jnp.concatenate(parts, axis=1)


def _select_topcap(aff, cap, tri):
    key = pltpu.bitcast(aff, I32)
    thr = jnp.zeros((aff.shape[0], 1), I32)
    for bit in range(30, -1, -1):
        cand = thr | (1 << bit)
        cnt = jnp.sum((key >= cand).astype(F32), axis=1, keepdims=True)
        thr = jnp.where(cnt >= cap, cand, thr)
    gt = (key > thr).astype(F32)
    eq = (key == thr).astype(F32)
    need = cap - jnp.sum(gt, axis=1, keepdims=True)
    sel = gt + eq * (_prefix_excl(eq, tri) < need).astype(F32)
    pos = _prefix_excl(sel, tri)
    return jnp.where(sel > 0.5, pos, -1.0)


def _route_body(aff_ref, code_ref, *, n_ctx, n_lat):
    aff = aff_ref[...]
    e = aff.shape[0]
    r = lax.broadcasted_iota(I32, (LANES, LANES), 0)
    c = lax.broadcasted_iota(I32, (LANES, LANES), 1)
    tri = (r < c).astype(BF16)
    cap_lat = EC_FACTOR * n_lat // N_EXPERTS
    code = _select_topcap(aff[:, n_ctx:n_ctx + n_lat], cap_lat, tri)
    if n_ctx:
        cap_ctx = EC_FACTOR * n_ctx // N_EXPERTS
        code_ctx = _select_topcap(aff[:, 0:n_ctx], cap_ctx, tri)
        code = jnp.concatenate([code_ctx, jnp.where(code >= 0.0, code + cap_ctx, -1.0)], axis=1)
    code = code.astype(I32)
    for ei in range(e):
        code_ref[ei] = code[ei:ei + 1, :]


def _route_call(aff, *, n_ctx):
    b, e, t = aff.shape
    return pl.pallas_call(
        functools.partial(_route_body, n_ctx=n_ctx, n_lat=t - n_ctx),
        grid=(b,),
        in_specs=[pl.BlockSpec((None, e, t), lambda bi: (bi, 0, 0))],
        out_specs=pl.BlockSpec((None, e, 1, t), lambda bi: (bi, 0, 0, 0)),
        out_shape=jax.ShapeDtypeStruct((b, e, 1, t), I32),
        compiler_params=_params(("parallel",)),
        name="route_select",
    )(aff)


def _moe_body(code_ref, aff_ref, h_ref, wg_ref, wu_ref, wd_ref, ye_ref, *, n_slots):
    code = code_ref[...]
    t = code.shape[1]
    hit = code == lax.broadcasted_iota(I32, (n_slots, t), 0)
    sel = jnp.where(hit, 1.0, 0.0).astype(BF16)
    xe = jnp.dot(sel, h_ref[...], preferred_element_type=F32).astype(BF16)
    gcol = jnp.sum(jnp.where(hit, aff_ref[...], 0.0), axis=1, keepdims=True)
    gate = jnp.dot(xe, wg_ref[...], preferred_element_type=F32)
    up = jnp.dot(xe, wu_ref[...], preferred_element_type=F32)
    hid = (gate * _sigmoid(gate) * up).astype(BF16)
    ye = jnp.dot(hid, wd_ref[...], preferred_element_type=F32) * gcol
    ye_ref[...] = ye.astype(BF16)


def _moe_call(code, aff4, h2, wg, wu, wd, *, n_slots):
    b, t, d = h2.shape
    e, _, f = wg.shape
    return pl.pallas_call(
        functools.partial(_moe_body, n_slots=n_slots),
        grid=(e, b),
        in_specs=[pl.BlockSpec((None, None, 1, t), lambda ei, bi: (bi, ei, 0, 0)),
                  pl.BlockSpec((None, None, 1, t), lambda ei, bi: (bi, ei, 0, 0)),
                  pl.BlockSpec((None, t, d), lambda ei, bi: (bi, 0, 0)),
                  pl.BlockSpec((None, d, f), lambda ei, bi: (ei, 0, 0)),
                  pl.BlockSpec((None, d, f), lambda ei, bi: (ei, 0, 0)),
                  pl.BlockSpec((None, f, d), lambda ei, bi: (ei, 0, 0))],
        out_specs=pl.BlockSpec((None, None, n_slots, d), lambda ei, bi: (bi, ei, 0, 0)),
        out_shape=jax.ShapeDtypeStruct((b, e, n_slots, d), BF16),
        compiler_params=_params(("arbitrary", "arbitrary")),
        name="moe_ffn",
    )(code, aff4, h2, wg, wu, wd)


def _combine_body(x_ref, code_ref, ye_ref, mod_ref, gf_ref, o_ref, *, n_slots, final):
    e = ye_ref.shape[0]
    tt = x_ref.shape[0]
    slot = lax.broadcasted_iota(I32, (n_slots, tt), 0)
    sel = jnp.concatenate(
        [jnp.where(code_ref[ei] == slot, 1.0, 0.0).astype(BF16) for ei in range(e)], axis=0)
    ye = ye_ref[...].reshape(e * n_slots, ye_ref.shape[2])
    acc = lax.dot_general(sel, ye, (((0,), (0,)), ((), ())), preferred_element_type=F32)
    xn = x_ref[...] + mod_ref[5:6, :] * acc
    if final:
        xn = _rmsnorm(xn, gf_ref[...])
    o_ref[...] = xn


def _combine_call(xc, code, ye, modcat, gfinal, *, n_ctx, tt, final):
    b, t, d = xc.shape
    e, n_slots = ye.shape[1], ye.shape[2]
    nct = n_ctx // tt
    return pl.pallas_call(
        functools.partial(_combine_body, n_slots=n_slots, final=final),
        grid=(b, t // tt),
        in_specs=[pl.BlockSpec((None, tt, d), lambda bi, ti: (bi, ti, 0)),
                  pl.BlockSpec((None, e, 1, tt), lambda bi, ti: (bi, 0, 0, ti)),
                  pl.BlockSpec((None, e, n_slots, d), lambda bi, ti: (bi, 0, 0, 0)),
                  pl.BlockSpec((None, None, N_MOD, d), lambda bi, ti: (bi, jnp.where(ti >= nct, 1, 0), 0, 0)),
                  pl.BlockSpec((1, d), lambda bi, ti: (0, 0))],
        out_specs=pl.BlockSpec((None, tt, d), lambda bi, ti: (bi, ti, 0)),
        out_shape=jax.ShapeDtypeStruct((b, t, d), F32),
        compiler_params=_params(("parallel", "arbitrary")),
        name="moe_combine",
    )(xc, code, ye, modcat, gfinal)


def _rope_tables(n_ctx, n_lat):
    quarter = HEAD_DIM // 4
    t = jnp.arange(n_lat, dtype=jnp.int32)
    rows = (t // GRID_W).astype(F32)
    cols = (t % GRID_W).astype(F32)
    freqs = ROPE_THETA ** (-jnp.arange(quarter, dtype=F32) / quarter)
    ang_r = rows[:, None] * freqs[None, :]
    ang_c = cols[:, None] * freqs[None, :]
    cos = jnp.concatenate([jnp.cos(ang_r), jnp.cos(ang_r), jnp.cos(ang_c), jnp.cos(ang_c)], axis=1)
    sin = jnp.concatenate([-jnp.sin(ang_r), jnp.sin(ang_r), -jnp.sin(ang_c), jnp.sin(ang_c)], axis=1)
    cos = jnp.concatenate([jnp.ones((n_ctx, HEAD_DIM), F32), cos], axis=0)
    sin = jnp.concatenate([jnp.zeros((n_ctx, HEAD_DIM), F32), sin], axis=0)
    return cos, sin


def kernel(x, c, ctx, c_ctx, ada_w, ada_b, norm_mix_g, w_in, attn_sink, conv_w, conv_b, lru_wa, lru_ba, lru_wx,
           lru_bx, lru_lambda, w_attn_br, w_rec_br, w_out, norm_ffn_g, w_router, w_gate, w_up, w_down,
           final_norm_g):
    b, n_lat, d = x.shape
    n_ctx = ctx.shape[1]
    depth = ada_w.shape[0]
    tt = math.gcd(math.gcd(n_ctx, n_lat), 256)

    pad_rows = (-(b + 1)) % SUBLANES
    cc = jnp.concatenate([c, c_ctx[None, :], jnp.zeros((pad_rows, d), F32)], axis=0)
    mod = _ada_call(cc, ada_w, ada_b)
    mod_lat = mod[:, :b].reshape(depth, b, 1, N_MOD, d)
    mod_ctx = jnp.broadcast_to(mod[:, b].reshape(depth, 1, 1, N_MOD, d), (depth, b, 1, N_MOD, d))
    modcat = jnp.concatenate([mod_ctx, mod_lat], axis=2)

    cos_t, sin_t = _rope_tables(n_ctx, n_lat)
    xc = jnp.concatenate([ctx, x], axis=1)

    w_in_bf = w_in.astype(BF16)
    wab_bf = w_attn_br.astype(BF16)
    wrb_bf = w_rec_br.astype(BF16)
    wo_bf = w_out.astype(BF16)
    wg_bf = w_gate.astype(BF16)
    wu_bf = w_up.astype(BF16)
    wd_bf = w_down.astype(BF16)
    wcat = jnp.concatenate([lru_wa[:, 0], lru_wx[:, 0], lru_wa[:, 1], lru_wx[:, 1]], axis=-1).astype(BF16)
    bcat = jnp.stack([lru_ba[:, 0], lru_bx[:, 0], lru_ba[:, 1], lru_bx[:, 1]], axis=1)
    bcat = bcat.reshape(depth, 4, LRU_BLOCKS, LANES).transpose(0, 2, 1, 3).reshape(depth, LRU_BLOCKS, 1, 4 * LANES)
    sink_rows = jnp.broadcast_to(attn_sink[:, :, None], (depth, N_HEADS, LANES))
    wrt = jnp.swapaxes(w_router, 1, 2)

    cap_lat = EC_FACTOR * n_lat // N_EXPERTS
    cap_ctx = EC_FACTOR * n_ctx // N_EXPERTS
    for l in range(depth):
        last = l == depth - 1
        with_ctx = not last
        q, k, v, u, z, ma, mr = _inproj_call(xc, modcat[l], norm_mix_g[l][None, :], w_in_bf[l], cos_t, sin_t,
                                             n_ctx=n_ctx, tt=tt)
        att = _attn_call(q, k, v, sink_rows[l], n_ctx=n_ctx, with_ctx=with_ctx)
        y = _lru_call(u, conv_w[l], conv_b[l][None, :], wcat[l], bcat[l], lru_lambda[l], n_ctx=n_ctx)
        xc, h2, aff = _merge_call(xc, att, y, z, ma, mr, modcat[l], norm_ffn_g[l][None, :],
                                  wab_bf[l], wrb_bf[l], wo_bf[l], wrt[l], n_ctx=n_ctx, tt=tt, with_ctx=with_ctx)
        ctx_rows = n_ctx if with_ctx else 0
        code = _route_call(aff, n_ctx=ctx_rows)
        n_slots = cap_lat + (cap_ctx if with_ctx else 0)
        ye = _moe_call(code, aff[:, :, None, :], h2, wg_bf[l], wu_bf[l], wd_bf[l], n_slots=n_slots)
        xc = _combine_call(xc, code, ye, modcat[l], final_norm_g[None, :], n_ctx=ctx_rows, tt=tt, final=last)
    return xc
```

```python
import functools
import math

import jax
import jax.numpy as jnp
from jax import lax
from jax.experimental import pallas as pl
from jax.experimental.pallas import tpu as pltpu

F32 = jnp.float32
BF16 = jnp.bfloat16
I32 = jnp.int32

HEAD_DIM = 128
N_HEADS = 8
N_KV_HEADS = 2
GROUP = N_HEADS // N_KV_HEADS
ATT_BLOCK = 128
GRID_W = 64
ROPE_THETA = 10000.0
LRU_BLOCKS = 8
LRU_C = 8.0
CONV_W = 4
CONV_LEFT = 2
N_EXPERTS = 16
EC_FACTOR = 2
N_MOD = 6
EPS = 1e-6
NEG_BIG = -1e30
LOG2E = 1.4426950408889634
QK_SCALE = HEAD_DIM ** -0.5 * LOG2E

LANES = 128
SUBLANES = 8
SCAN_SEGMENTS = 64
SCAN_UNROLL = 4
MIN_NORMAL = 2.0 ** -126
MANTISSA_STEPS = 32
VMEM_LIMIT = 56 * 1024 * 1024


def _sigmoid(x):
    return 0.5 * jnp.tanh(0.5 * x) + 0.5


def _gelu_tanh(x):
    c = math.sqrt(2.0 / math.pi)
    return 0.5 * x * (1.0 + jnp.tanh(c * (x + 0.044715 * (x * x * x))))


def _rmsnorm(x, g):
    ms = jnp.mean(x * x, axis=-1, keepdims=True)
    return x * lax.rsqrt(ms + EPS) * g


def _params(sem):
    return pltpu.CompilerParams(dimension_semantics=sem, vmem_limit_bytes=VMEM_LIMIT)


def _ada_body(c_ref, w_ref, b_ref, o_ref):
    c = c_ref[...]
    act = c * _sigmoid(c)
    o_ref[...] = jnp.dot(act, w_ref[...], preferred_element_type=F32,
                         precision=lax.Precision.HIGHEST) + b_ref[...]


def _ada_call(cc, ada_w, ada_b):
    depth, d, n6 = ada_w.shape
    rows = cc.shape[0]
    tn = 1024
    return pl.pallas_call(
        _ada_body,
        grid=(depth, n6 // tn),
        in_specs=[pl.BlockSpec((rows, d), lambda l, j: (0, 0)),
                  pl.BlockSpec((None, d, tn), lambda l, j: (l, 0, j)),
                  pl.BlockSpec((None, 1, tn), lambda l, j: (l, 0, j))],
        out_specs=pl.BlockSpec((None, rows, tn), lambda l, j: (l, 0, j)),
        out_shape=jax.ShapeDtypeStruct((depth, rows, n6), F32),
        compiler_params=_params(("parallel", "parallel")),
        name="ada_mod",
    )(cc, ada_w, ada_b.reshape(depth, 1, n6))


def _rope(xh, cos, ssin, hi):
    sw = jnp.where(hi, pltpu.roll(xh, 32, 1), pltpu.roll(xh, LANES - 32, 1))
    return xh * cos + sw * ssin


def _inproj_body(x_ref, mod_ref, g_ref, w_ref, cos_ref, sin_ref,
                 q_ref, k_ref, v_ref, u_ref, z_ref, ma_ref, mr_ref, *, d):
    x = x_ref[...]
    h = _rmsnorm(x, g_ref[...]) * (1.0 + mod_ref[1:2, :]) + mod_ref[0:1, :]
    hb = h.astype(BF16)
    cos = cos_ref[...]
    ssin = sin_ref[...]
    cos_q = cos * QK_SCALE
    ssin_q = ssin * QK_SCALE
    hi = (lax.broadcasted_iota(I32, cos.shape, 1) & 32) != 0
    qw = N_HEADS * HEAD_DIM
    kw = N_KV_HEADS * HEAD_DIM

    def proj(lo, width):
        return jnp.dot(hb, w_ref[:, lo:lo + width], preferred_element_type=F32)

    for hh in range(N_HEADS):
        qh = proj(hh * HEAD_DIM, HEAD_DIM)
        q_ref[:, hh * HEAD_DIM:(hh + 1) * HEAD_DIM] = _rope(qh, cos_q, ssin_q, hi).astype(BF16)
    for hh in range(N_KV_HEADS):
        kh = proj(qw + hh * HEAD_DIM, HEAD_DIM)
        k_ref[:, hh * HEAD_DIM:(hh + 1) * HEAD_DIM] = _rope(kh, cos, ssin, hi).astype(BF16)
    off = qw + kw
    v_ref[...] = proj(off, kw).astype(BF16)
    off += kw
    u_ref[...] = proj(off, d)
    off += d
    for c0 in range(0, d, 2 * LANES):
        z_ref[:, c0:c0 + 2 * LANES] = _gelu_tanh(proj(off + c0, 2 * LANES)).astype(BF16)
    off += d
    for c0 in range(0, d, 2 * LANES):
        ma_ref[:, c0:c0 + 2 * LANES] = _sigmoid(proj(off + c0, 2 * LANES)).astype(BF16)
    off += d
    for c0 in range(0, d, 2 * LANES):
        mr_ref[:, c0:c0 + 2 * LANES] = _sigmoid(proj(off + c0, 2 * LANES)).astype(BF16)


def _inproj_call(l, xc, modcat, g, w_in_bf, cos_t, sin_t, *, n_ctx, tt):
    b, t, d = xc.shape
    in_w = w_in_bf.shape[2]
    nct = n_ctx // tt
    qw = N_HEADS * HEAD_DIM
    kw = N_KV_HEADS * HEAD_DIM
    row = lambda bi, ti: (bi, ti, 0)
    outs = [jax.ShapeDtypeStruct((b, t, qw), BF16), jax.ShapeDtypeStruct((b, t, kw), BF16),
            jax.ShapeDtypeStruct((b, t, kw), BF16), jax.ShapeDtypeStruct((b, t, d), F32),
            jax.ShapeDtypeStruct((b, t, d), BF16), jax.ShapeDtypeStruct((b, t, d), BF16),
            jax.ShapeDtypeStruct((b, t, d), BF16)]
    return pl.pallas_call(
        functools.partial(_inproj_body, d=d),
        grid=(b, t // tt),
        in_specs=[pl.BlockSpec((None, tt, d), row),
                  pl.BlockSpec((None, None, None, N_MOD, d),
                               lambda bi, ti: (l, bi, jnp.where(ti >= nct, 1, 0), 0, 0)),
                  pl.BlockSpec((None, 1, d), lambda bi, ti: (l, 0, 0)),
                  pl.BlockSpec((None, d, in_w), lambda bi, ti: (l, 0, 0)),
                  pl.BlockSpec((tt, HEAD_DIM), lambda bi, ti: (ti, 0)),
                  pl.BlockSpec((tt, HEAD_DIM), lambda bi, ti: (ti, 0))],
        out_specs=[pl.BlockSpec((None, tt, s.shape[2]), row) for s in outs],
        out_shape=outs,
        compiler_params=_params(("parallel", "parallel")),
        name="in_proj",
    )(xc, modcat, g, w_in_bf, cos_t, sin_t)


def _attn_body(q_ref, k_ref, v_ref, sink_ref, o_ref, *, n_ctx, t_all, blk_off):
    n = pl.program_id(1) + blk_off
    nc = n_ctx // ATT_BLOCK
    band = 3 * ATT_BLOCK
    start = jnp.clip(n_ctx + (n - nc - 1) * ATT_BLOCK, 0, t_all - band)
    start = pl.multiple_of(start, ATT_BLOCK)

    kpos = start - n_ctx + lax.broadcasted_iota(I32, (band, ATT_BLOCK), 0)
    qpos = (n - nc) * ATT_BLOCK + lax.broadcasted_iota(I32, (band, ATT_BLOCK), 1)
    valid = (kpos >= 0) & (jnp.abs(qpos - kpos) <= ATT_BLOCK) & (n >= nc)
    bias = jnp.where(valid, 0.0, NEG_BIG)
    bias = jnp.concatenate([bias] * GROUP, axis=1)

    nt = (((1,), (1,)), ((), ()))
    tn = (((0,), (0,)), ((), ()))
    for kh in range(N_KV_HEADS):
        lanes = slice(kh * HEAD_DIM, (kh + 1) * HEAD_DIM)
        qs = jnp.concatenate(
            [q_ref[:, (kh * GROUP + g) * HEAD_DIM:(kh * GROUP + g + 1) * HEAD_DIM] for g in range(GROUP)], axis=0)
        kb = k_ref[pl.ds(start, band), lanes]
        vb = v_ref[pl.ds(start, band), lanes]
        kc = k_ref[0:n_ctx, lanes]
        vc = v_ref[0:n_ctx, lanes]
        s_band = lax.dot_general(kb, qs, nt, preferred_element_type=F32) + bias
        s_ctx = lax.dot_general(kc, qs, nt, preferred_element_type=F32)
        sink = jnp.concatenate(
            [sink_ref[kh * GROUP + g:kh * GROUP + g + 1, :] for g in range(GROUP)], axis=1) * LOG2E
        m = jnp.maximum(sink, jnp.maximum(jnp.max(s_band, axis=0, keepdims=True),
                                          jnp.max(s_ctx, axis=0, keepdims=True)))
        p_band = jnp.exp2(s_band - m)
        p_ctx = jnp.exp2(s_ctx - m)
        denom = (jnp.exp2(sink - m) + jnp.sum(p_band, axis=0, keepdims=True)
                 + jnp.sum(p_ctx, axis=0, keepdims=True))
        o_t = (lax.dot_general(vb, p_band.astype(BF16), tn, preferred_element_type=F32)
               + lax.dot_general(vc, p_ctx.astype(BF16), tn, preferred_element_type=F32)) / denom
        for g in range(GROUP):
            hh = kh * GROUP + g
            o_ref[:, hh * HEAD_DIM:(hh + 1) * HEAD_DIM] = o_t[:, g * ATT_BLOCK:(g + 1) * ATT_BLOCK].T.astype(BF16)


def _attn_call(l, q, k, v, sink_rows, *, n_ctx, with_ctx):
    b, t, qw = q.shape
    kw = k.shape[2]
    blk_off = 0 if with_ctx else n_ctx // ATT_BLOCK
    nblk = t // ATT_BLOCK - blk_off
    return pl.pallas_call(
        functools.partial(_attn_body, n_ctx=n_ctx, t_all=t, blk_off=blk_off),
        grid=(b, nblk),
        in_specs=[pl.BlockSpec((None, ATT_BLOCK, qw), lambda bi, ni: (bi, ni + blk_off, 0)),
                  pl.BlockSpec((None, t, kw), lambda bi, ni: (bi, 0, 0)),
                  pl.BlockSpec((None, t, kw), lambda bi, ni: (bi, 0, 0)),
                  pl.BlockSpec((None, N_HEADS, LANES), lambda bi, ni: (l, 0, 0))],
        out_specs=pl.BlockSpec((None, ATT_BLOCK, qw), lambda bi, ni: (bi, ni + blk_off, 0)),
        out_shape=jax.ShapeDtypeStruct((b, t, qw), BF16),
        compiler_params=_params(("parallel", "arbitrary")),
        name="win_attn",
    )(q, k, v, sink_rows)


def _lru_body(u_ref, cw_ref, cb_ref, w_ref, b_ref, lam_ref, y_ref,
              upad, a_f, b_f, a_b, b_b, h_f, h_b, *, n_ctx, n_lat, chunk):
    t_all = n_ctx + n_lat
    pad = SUBLANES
    zeros_pad = jnp.zeros((pad, LANES), F32)
    upad[0:pad, :] = zeros_pad
    upad[pad:pad + n_ctx, :] = u_ref[0:n_ctx, :]
    upad[pad + n_ctx:2 * pad + n_ctx, :] = zeros_pad
    upad[2 * pad + n_ctx:2 * pad + t_all, :] = u_ref[n_ctx:t_all, :]
    upad[2 * pad + t_all:3 * pad + t_all, :] = zeros_pad

    lam = lam_ref[...]
    nl = -lam
    sp = jnp.maximum(nl, 0.0) + jnp.log(1.0 + jnp.exp(-jnp.abs(nl)))
    c2 = sp * (-0.5 * LRU_C * LOG2E)
    cw = cw_ref[...]
    cb = cb_ref[...]
    wcat = w_ref[...]
    bcat = b_ref[...]

    def gate_chunk(ci, carry):
        r0 = pl.multiple_of(ci * chunk, chunk)
        in_lat = r0 >= n_ctx
        p0 = r0 + pad + jnp.where(in_lat, pad, 0)
        win = upad[pl.ds(pl.multiple_of(p0 - pad, pad), chunk + 2 * pad), :]
        uc = cb
        for kk in range(CONV_W):
            lo = pad + kk - CONV_LEFT
            uc = uc + cw[kk:kk + 1, :] * win[lo:lo + chunk, :]
        th = jnp.tanh(jnp.dot(uc.astype(BF16), wcat, preferred_element_type=F32) + bcat)
        uh = 0.5 * uc
        dest_b = pl.multiple_of(jnp.where(in_lat, r0 - n_ctx, r0 + n_lat), chunk)
        for dr, (a_ref, bb_ref, dest) in enumerate(((a_f, b_f, r0), (a_b, b_b, dest_b))):
            t_r = th[:, (2 * dr) * LANES:(2 * dr + 1) * LANES]
            t_i = th[:, (2 * dr + 1) * LANES:(2 * dr + 2) * LANES]
            cc = c2[dr:dr + 1, :]
            a = jnp.exp2(t_r * cc + cc)
            om = 1.0 - a * a
            root = jnp.where(om > 0.0, om * lax.rsqrt(om), 0.0)
            a_ref[pl.ds(dest, chunk), :] = a
            bb_ref[pl.ds(dest, chunk), :] = root * (t_i * uh + uh)
        return carry

    lax.fori_loop(0, t_all // chunk, gate_chunk, 0)

    nseg = SCAN_SEGMENTS
    seg = t_all // nseg

    def rows(i):
        return pl.ds(i, nseg, stride=seg)

    ones = jnp.ones((nseg, LANES), F32)
    zeros = jnp.zeros((nseg, LANES), F32)

    def pass1(i, carry):
        hf, pf, hb, pb = carry
        j = seg - 1 - i
        af = a_f[rows(i), :]
        hf = af * hf + b_f[rows(i), :]
        pf = af * pf
        h_f[rows(i), :] = hf
        ab = a_b[rows(j), :]
        hb = ab * hb + b_b[rows(j), :]
        pb = ab * pb
        h_b[rows(j), :] = hb
        return hf, pf, hb, pb

    hf_fin, pf_tot, hb_fin, pb_tot = lax.fori_loop(0, seg, pass1, (zeros, ones, zeros, ones), unroll=SCAN_UNROLL)

    hin = jnp.zeros((1, LANES), F32)
    hin_f = []
    for s in range(nseg):
        hin_f.append(hin)
        hin = hf_fin[s:s + 1, :] + pf_tot[s:s + 1, :] * hin
    hin = jnp.zeros((1, LANES), F32)
    hin_b = [None] * nseg
    for s in range(nseg - 1, -1, -1):
        hin_b[s] = hin
        hin = hb_fin[s:s + 1, :] + pb_tot[s:s + 1, :] * hin
    hin_f = jnp.concatenate(hin_f, axis=0)
    hin_b = jnp.concatenate(hin_b, axis=0)

    def pass2(i, carry):
        pf, pb = carry
        j = seg - 1 - i
        pf = a_f[rows(i), :] * pf
        h_f[rows(i), :] = h_f[rows(i), :] + pf * hin_f
        pb = a_b[rows(j), :] * pb
        h_b[rows(j), :] = h_b[rows(j), :] + pb * hin_b
        return pf, pb

    lax.fori_loop(0, seg, pass2, (ones, ones), unroll=SCAN_UNROLL)

    y_ref[0:n_ctx, :] = (h_f[0:n_ctx, :] + h_b[n_lat:t_all, :]).astype(BF16)
    y_ref[n_ctx:t_all, :] = (h_f[n_ctx:t_all, :] + h_b[0:n_lat, :]).astype(BF16)


def _lru_call(l, u, conv_w, conv_b, wcat, bcat, lam, *, n_ctx):
    b, t, w = u.shape
    n_lat = t - n_ctx
    chunk = math.gcd(math.gcd(n_ctx, n_lat), 256)
    assert t % SCAN_SEGMENTS == 0 and w == LRU_BLOCKS * LANES
    scr = [pltpu.VMEM((t + 3 * SUBLANES, LANES), F32)] + [pltpu.VMEM((t, LANES), F32) for _ in range(6)]
    return pl.pallas_call(
        functools.partial(_lru_body, n_ctx=n_ctx, n_lat=n_lat, chunk=chunk),
        grid=(b, LRU_BLOCKS),
        in_specs=[pl.BlockSpec((None, t, LANES), lambda bi, ci: (bi, 0, ci)),
                  pl.BlockSpec((None, CONV_W, LANES), lambda bi, ci: (l, 0, ci)),
                  pl.BlockSpec((None, 1, LANES), lambda bi, ci: (l, 0, ci)),
                  pl.BlockSpec((None, None, LANES, 4 * LANES), lambda bi, ci: (l, ci, 0, 0)),
                  pl.BlockSpec((None, None, 1, 4 * LANES), lambda bi, ci: (l, ci, 0, 0)),
                  pl.BlockSpec((None, 2, LANES), lambda bi, ci: (l, 0, ci))],
        out_specs=pl.BlockSpec((None, t, LANES), lambda bi, ci: (bi, 0, ci)),
        out_shape=jax.ShapeDtypeStruct((b, t, w), BF16),
        scratch_shapes=scr,
        compiler_params=_params(("parallel", "parallel")),
        name="rg_lru",
    )(u, conv_w, conv_b, wcat, bcat, lam)


def _merge_body(x_ref, att_ref, y_ref, gz_ref, sa_ref, sr_ref, mod_ref, g_ref,
                wab_ref, wrb_ref, wo_ref, wrh_ref, wrl_ref, xo_ref, h2_ref, aff_ref, mix_s):
    tt, d = x_ref.shape
    ch = 2 * LANES
    n_exp = aff_ref.shape[0]
    att = att_ref[...]
    rec_in = y_ref[...] * gz_ref[...]
    for c0 in range(0, d, ch):
        cols = slice(c0, c0 + ch)
        att_d = jnp.dot(att, wab_ref[:, cols], preferred_element_type=F32)
        rec_d = jnp.dot(rec_in, wrb_ref[:, cols], preferred_element_type=F32)
        mix = sa_ref[:, cols].astype(F32) * att_d + sr_ref[:, cols].astype(F32) * rec_d
        mix_s[:, cols] = mix.astype(BF16)
    mix = mix_s[...]
    part = jnp.zeros((tt, LANES), F32)
    for c0 in range(0, d, ch):
        cols = slice(c0, c0 + ch)
        out = jnp.dot(mix, wo_ref[:, cols], preferred_element_type=F32)
        xn = x_ref[:, cols] + mod_ref[2:3, cols] * out
        xo_ref[:, cols] = xn
        sq = xn * xn
        part = part + sq[:, 0:LANES] + sq[:, LANES:ch]
    rs = lax.rsqrt(jnp.sum(part, axis=-1, keepdims=True) * (1.0 / d) + EPS)
    logits = jnp.zeros((tt, LANES), F32)
    for c0 in range(0, d, ch):
        cols = slice(c0, c0 + ch)
        h2 = xo_ref[:, cols] * rs * g_ref[:, cols] * (1.0 + mod_ref[4:5, cols]) + mod_ref[3:4, cols]
        hi = h2.astype(BF16)
        h2_ref[:, cols] = hi
        lo = (h2 - hi.astype(F32)).astype(BF16)
        logits = (logits + jnp.dot(hi, wrh_ref[cols, :], preferred_element_type=F32)
                  + jnp.dot(lo, wrh_ref[cols, :], preferred_element_type=F32)
                  + jnp.dot(hi, wrl_ref[cols, :], preferred_element_type=F32))
    lt = logits.T[0:n_exp, :]
    mx = jnp.max(lt, axis=0, keepdims=True)
    ex = jnp.exp(lt - mx)
    aff_ref[...] = ex / jnp.sum(ex, axis=0, keepdims=True)


def _merge_call(l, xc, att, y, gz, sa, sr, modcat, g, wab, wrb, wo, wr_hi, wr_lo, *, n_ctx, tt, with_ctx):
    b, t, d = xc.shape
    nct = n_ctx // tt
    off = 0 if with_ctx else nct
    t_out = t - off * tt
    row = lambda bi, ti: (bi, ti + off, 0)
    orow = lambda bi, ti: (bi, ti, 0)
    lay = lambda bi, ti: (l, 0, 0)
    return pl.pallas_call(
        _merge_body,
        grid=(b, t // tt - off),
        in_specs=[pl.BlockSpec((None, tt, d), row)] * 6 + [
            pl.BlockSpec((None, None, None, N_MOD, d),
                         lambda bi, ti: (l, bi, jnp.where(ti + off >= nct, 1, 0), 0, 0)),
            pl.BlockSpec((None, 1, d), lay),
            pl.BlockSpec((None,) + wab.shape[1:], lay), pl.BlockSpec((None,) + wrb.shape[1:], lay),
            pl.BlockSpec((None,) + wo.shape[1:], lay),
            pl.BlockSpec((None, d, LANES), lay), pl.BlockSpec((None, d, LANES), lay)],
        out_specs=[pl.BlockSpec((None, tt, d), orow), pl.BlockSpec((None, tt, d), orow),
                   pl.BlockSpec((None, N_EXPERTS, tt), lambda bi, ti: (bi, 0, ti))],
        out_shape=[jax.ShapeDtypeStruct((b, t_out, d), F32), jax.ShapeDtypeStruct((b, t_out, d), BF16),
                   jax.ShapeDtypeStruct((b, N_EXPERTS, t_out), F32)],
        scratch_shapes=[pltpu.VMEM((tt, d), BF16)],
        compiler_params=_params(("parallel", "parallel")),
        name="merge_router",
    )(xc, att, y, gz, sa, sr, modcat, g, wab, wrb, wo, wr_hi, wr_lo)


def _prefix_excl(mask, tri):
    rows, length = mask.shape
    run = jnp.zeros((rows, 1), F32)
    parts = []
    for kb in range(length // LANES):
        blk = mask[:, kb * LANES:(kb + 1) * LANES]
        loc = jnp.dot(blk.astype(BF16), tri, preferred_element_type=F32)
        parts.append(loc + run)
        run = run + jnp.sum(blk, axis=1, keepdims=True)
    return jnp.concatenate(parts, axis=1)


def _count_ge(aff, cand):
    return jnp.sum(jnp.where(aff >= cand, 1.0, 0.0), axis=1, keepdims=True)


def _select_topcap(segments, tri):
    n = len(segments)
    affs = [s[0] for s in segments]
    caps = [float(s[1]) for s in segments]
    rows = affs[0].shape[0]
    base = jnp.full((rows, 1), MIN_NORMAL, F32)
    has = [_count_ge(affs[i], base) >= caps[i] for i in range(n)]
    lo = [base] * n
    for j in range(6, -1, -1):
        f = float(2.0 ** (2 ** j))
        for i in range(n):
            cand = lo[i] * f
            lo[i] = jnp.where(_count_ge(affs[i], cand) >= caps[i], cand, lo[i])
    lo = [jnp.where(has[i], lo[i], 0.0) for i in range(n)]
    cur = list(lo)
    for j in range(1, MANTISSA_STEPS + 1):
        for i in range(n):
            cand = cur[i] + lo[i] * float(2.0 ** -j)
            cur[i] = jnp.where(_count_ge(affs[i], cand) >= caps[i], cand, cur[i])
    codes = []
    for i in range(n):
        aff = affs[i]
        nxt = jnp.where(has[i], cur[i] + lo[i] * float(2.0 ** -23), base)
        ge = jnp.where(aff >= cur[i], 1.0, 0.0)
        tie = ge * jnp.where(aff < nxt, 1.0, 0.0)
        excess = jnp.sum(ge, axis=1, keepdims=True) - caps[i]
        after = jnp.sum(tie, axis=1, keepdims=True) - (_prefix_excl(tie, tri) + tie)
        sel = ge - tie * jnp.where(after < excess, 1.0, 0.0)
        pos = _prefix_excl(sel, tri)
        codes.append(jnp.where(sel > 0.5, pos, -1.0))
    return codes


def _route_body(aff_ref, code_ref, affrow_ref, *, n_ctx, n_lat):
    aff = aff_ref[...]
    e = aff.shape[0]
    r = lax.broadcasted_iota(I32, (LANES, LANES), 0)
    c = lax.broadcasted_iota(I32, (LANES, LANES), 1)
    tri = (r < c).astype(BF16)
    cap_lat = EC_FACTOR * n_lat // N_EXPERTS
    if n_ctx:
        cap_ctx = EC_FACTOR * n_ctx // N_EXPERTS
        code_ctx, code = _select_topcap([(aff[:, 0:n_ctx], cap_ctx), (aff[:, n_ctx:n_ctx + n_lat], cap_lat)], tri)
        code = jnp.concatenate([code_ctx, jnp.where(code >= 0.0, code + cap_ctx, -1.0)], axis=1)
    else:
        (code,) = _select_topcap([(aff, cap_lat)], tri)
    code = code.astype(I32)
    for ei in range(e):
        code_ref[ei] = code[ei:ei + 1, :]
        affrow_ref[ei] = aff[ei:ei + 1, :]


def _route_call(aff, *, n_ctx):
    b, e, t = aff.shape
    spec4 = pl.BlockSpec((None, e, 1, t), lambda bi: (bi, 0, 0, 0))
    return pl.pallas_call(
        functools.partial(_route_body, n_ctx=n_ctx, n_lat=t - n_ctx),
        grid=(b,),
        in_specs=[pl.BlockSpec((None, e, t), lambda bi: (bi, 0, 0))],
        out_specs=[spec4, spec4],
        out_shape=[jax.ShapeDtypeStruct((b, e, 1, t), I32), jax.ShapeDtypeStruct((b, e, 1, t), F32)],
        compiler_params=_params(("parallel",)),
        name="route_select",
    )(aff)


def _moe_body(code_ref, aff_ref, h_ref, wg_ref, wu_ref, wd_ref, ye_ref, *, n_slots):
    code = code_ref[...]
    t = code.shape[1]
    hit = code == lax.broadcasted_iota(I32, (n_slots, t), 0)
    sel = jnp.where(hit, 1.0, 0.0).astype(BF16)
    xe = jnp.dot(sel, h_ref[...], preferred_element_type=F32).astype(BF16)
    gcol = jnp.sum(jnp.where(hit, aff_ref[...], 0.0), axis=1, keepdims=True)
    gate = jnp.dot(xe, wg_ref[...], preferred_element_type=F32)
    up = jnp.dot(xe, wu_ref[...], preferred_element_type=F32)
    hid = (gate * _sigmoid(gate) * up).astype(BF16)
    ye = jnp.dot(hid, wd_ref[...], preferred_element_type=F32) * gcol
    ye_ref[...] = ye.astype(BF16)


def _moe_call(l, code, aff4, h2, wg, wu, wd, *, n_slots):
    b, t, d = h2.shape
    _, e, _, f = wg.shape
    return pl.pallas_call(
        functools.partial(_moe_body, n_slots=n_slots),
        grid=(e, b),
        in_specs=[pl.BlockSpec((None, None, 1, t), lambda ei, bi: (bi, ei, 0, 0)),
                  pl.BlockSpec((None, None, 1, t), lambda ei, bi: (bi, ei, 0, 0)),
                  pl.BlockSpec((None, t, d), lambda ei, bi: (bi, 0, 0)),
                  pl.BlockSpec((None, None, d, f), lambda ei, bi: (l, ei, 0, 0)),
                  pl.BlockSpec((None, None, d, f), lambda ei, bi: (l, ei, 0, 0)),
                  pl.BlockSpec((None, None, f, d), lambda ei, bi: (l, ei, 0, 0))],
        out_specs=pl.BlockSpec((None, None, n_slots, d), lambda ei, bi: (bi, ei, 0, 0)),
        out_shape=jax.ShapeDtypeStruct((b, e, n_slots, d), BF16),
        compiler_params=_params(("arbitrary", "arbitrary")),
        name="moe_ffn",
    )(code, aff4, h2, wg, wu, wd)


def _combine_body(x_ref, code_ref, ye_ref, mod_ref, gf_ref, o_ref, *, n_slots, final):
    e = ye_ref.shape[0]
    tt = x_ref.shape[0]
    slot = lax.broadcasted_iota(I32, (n_slots, tt), 0)
    sel = jnp.concatenate(
        [jnp.where(code_ref[ei] == slot, 1.0, 0.0).astype(BF16) for ei in range(e)], axis=0)
    ye = ye_ref[...].reshape(e * n_slots, ye_ref.shape[2])
    acc = lax.dot_general(sel, ye, (((0,), (0,)), ((), ())), preferred_element_type=F32)
    xn = x_ref[...] + mod_ref[5:6, :] * acc
    if final:
        xn = _rmsnorm(xn, gf_ref[...])
    o_ref[...] = xn


def _combine_call(l, xc, code, ye, modcat, gfinal, *, n_ctx, tt, final):
    b, t, d = xc.shape
    e, n_slots = ye.shape[1], ye.shape[2]
    nct = n_ctx // tt
    return pl.pallas_call(
        functools.partial(_combine_body, n_slots=n_slots, final=final),
        grid=(b, t // tt),
        in_specs=[pl.BlockSpec((None, tt, d), lambda bi, ti: (bi, ti, 0)),
                  pl.BlockSpec((None, e, 1, tt), lambda bi, ti: (bi, 0, 0, ti)),
                  pl.BlockSpec((None, e, n_slots, d), lambda bi, ti: (bi, 0, 0, 0)),
                  pl.BlockSpec((None, None, None, N_MOD, d),
                               lambda bi, ti: (l, bi, jnp.where(ti >= nct, 1, 0), 0, 0)),
                  pl.BlockSpec((1, d), lambda bi, ti: (0, 0))],
        out_specs=pl.BlockSpec((None, tt, d), lambda bi, ti: (bi, ti, 0)),
        out_shape=jax.ShapeDtypeStruct((b, t, d), F32),
        compiler_params=_params(("parallel", "arbitrary")),
        name="moe_combine",
    )(xc, code, ye, modcat, gfinal)


def _rope_tables(n_ctx, n_lat):
    quarter = HEAD_DIM // 4
    t = jnp.arange(n_lat, dtype=jnp.int32)
    rows = (t // GRID_W).astype(F32)
    cols = (t % GRID_W).astype(F32)
    freqs = ROPE_THETA ** (-jnp.arange(quarter, dtype=F32) / quarter)
    ang_r = rows[:, None] * freqs[None, :]
    ang_c = cols[:, None] * freqs[None, :]
    cos = jnp.concatenate([jnp.cos(ang_r), jnp.cos(ang_r), jnp.cos(ang_c), jnp.cos(ang_c)], axis=1)
    sin = jnp.concatenate([-jnp.sin(ang_r), jnp.sin(ang_r), -jnp.sin(ang_c), jnp.sin(ang_c)], axis=1)
    cos = jnp.concatenate([jnp.ones((n_ctx, HEAD_DIM), F32), cos], axis=0)
    sin = jnp.concatenate([jnp.zeros((n_ctx, HEAD_DIM), F32), sin], axis=0)
    return cos, sin


def kernel(x, c, ctx, c_ctx, ada_w, ada_b, norm_mix_g, w_in, attn_sink, conv_w, conv_b, lru_wa, lru_ba, lru_wx,
           lru_bx, lru_lambda, w_attn_br, w_rec_br, w_out, norm_ffn_g, w_router, w_gate, w_up, w_down,
           final_norm_g):
    b, n_lat, d = x.shape
    n_ctx = ctx.shape[1]
    depth = ada_w.shape[0]
    tt = math.gcd(math.gcd(n_ctx, n_lat), 256)

    pad_rows = (-(b + 1)) % SUBLANES
    cc = jnp.concatenate([c, c_ctx[None, :], jnp.zeros((pad_rows, d), F32)], axis=0)
    mod = _ada_call(cc, ada_w, ada_b)
    mod_lat = mod[:, :b].reshape(depth, b, 1, N_MOD, d)
    mod_ctx = jnp.broadcast_to(mod[:, b].reshape(depth, 1, 1, N_MOD, d), (depth, b, 1, N_MOD, d))
    modcat = jnp.concatenate([mod_ctx, mod_lat], axis=2)

    cos_t, sin_t = _rope_tables(n_ctx, n_lat)
    xc = jnp.concatenate([ctx, x], axis=1)

    w_in_bf = w_in.astype(BF16)
    wab_bf = w_attn_br.astype(BF16)
    wrb_bf = w_rec_br.astype(BF16)
    wo_bf = w_out.astype(BF16)
    wg_bf = w_gate.astype(BF16)
    wu_bf = w_up.astype(BF16)
    wd_bf = w_down.astype(BF16)
    wcat = (0.5 * jnp.concatenate([lru_wa[:, 0], lru_wx[:, 0], lru_wa[:, 1], lru_wx[:, 1]], axis=-1)).astype(BF16)
    bcat = 0.5 * jnp.stack([lru_ba[:, 0], lru_bx[:, 0], lru_ba[:, 1], lru_bx[:, 1]], axis=1)
    bcat = bcat.reshape(depth, 4, LRU_BLOCKS, LANES).transpose(0, 2, 1, 3).reshape(depth, LRU_BLOCKS, 1, 4 * LANES)
    sink_rows = jnp.broadcast_to(attn_sink[:, :, None], (depth, N_HEADS, LANES))
    wr_pad = jnp.pad(w_router, ((0, 0), (0, 0), (0, LANES - w_router.shape[2])))
    wr_hi = wr_pad.astype(BF16)
    wr_lo = (wr_pad - wr_hi.astype(F32)).astype(BF16)
    g_mix = norm_mix_g[:, None, :]
    g_ffn = norm_ffn_g[:, None, :]
    conv_b3 = conv_b[:, None, :]

    cap_lat = EC_FACTOR * n_lat // N_EXPERTS
    cap_ctx = EC_FACTOR * n_ctx // N_EXPERTS
    for l in range(depth):
        last = l == depth - 1
        with_ctx = not last
        q, k, v, u, gz, sa, sr = _inproj_call(l, xc, modcat, g_mix, w_in_bf, cos_t, sin_t, n_ctx=n_ctx, tt=tt)
        att = _attn_call(l, q, k, v, sink_rows, n_ctx=n_ctx, with_ctx=with_ctx)
        y = _lru_call(l, u, conv_w, conv_b3, wcat, bcat, lru_lambda, n_ctx=n_ctx)
        xc, h2, aff = _merge_call(l, xc, att, y, gz, sa, sr, modcat, g_ffn, wab_bf, wrb_bf, wo_bf, wr_hi, wr_lo,
                                  n_ctx=n_ctx, tt=tt, with_ctx=with_ctx)
        ctx_rows = n_ctx if with_ctx else 0
        code, aff_rows = _route_call(aff, n_ctx=ctx_rows)
        n_slots = cap_lat + (cap_ctx if with_ctx else 0)
        ye = _moe_call(l, code, aff_rows, h2, wg_bf, wu_bf, wd_bf, n_slots=n_slots)
        xc = _combine_call(l, xc, code, ye, modcat, final_norm_g[None, :], n_ctx=ctx_rows, tt=tt, final=last)
    return xc
```

```python
import functools
import math

import jax
import jax.numpy as jnp
from jax import lax
from jax.experimental import pallas as pl
from jax.experimental.pallas import tpu as pltpu

F32 = jnp.float32
BF16 = jnp.bfloat16
I32 = jnp.int32

HEAD_DIM = 128
N_HEADS = 8
N_KV_HEADS = 2
GROUP = N_HEADS // N_KV_HEADS
ATT_BLOCK = 128
GRID_W = 64
ROPE_THETA = 10000.0
LRU_BLOCKS = 8
LRU_C = 8.0
CONV_W = 4
CONV_LEFT = 2
N_EXPERTS = 16
EC_FACTOR = 2
N_MOD = 6
EPS = 1e-6
NEG_BIG = -1e30
LOG2E = 1.4426950408889634
QK_SCALE = HEAD_DIM ** -0.5 * LOG2E

LANES = 128
SUBLANES = 8
SCAN_SEGMENTS = 64
SCAN_UNROLL = 4
INPROJ_MAX_ROWS = 768
ATT_BLOCKS_PER_STEP = 2
COMBINE_WINDOW = 96
MIN_NORMAL = 2.0 ** -126
MANTISSA_STEPS = 32
VMEM_LIMIT = 56 * 1024 * 1024


def _sigmoid(x):
    return 0.5 * jnp.tanh(0.5 * x) + 0.5


def _gelu_tanh(x):
    c = math.sqrt(2.0 / math.pi)
    return 0.5 * x * (1.0 + jnp.tanh(c * (x + 0.044715 * (x * x * x))))


def _rmsnorm(x, g):
    ms = jnp.mean(x * x, axis=-1, keepdims=True)
    return x * lax.rsqrt(ms + EPS) * g


def _params(sem):
    return pltpu.CompilerParams(dimension_semantics=sem, vmem_limit_bytes=VMEM_LIMIT)


def _ada_body(c_ref, w_ref, b_ref, o_ref):
    c = c_ref[...]
    act = c * _sigmoid(c)
    o_ref[...] = jnp.dot(act, w_ref[...], preferred_element_type=F32,
                         precision=lax.Precision.HIGHEST) + b_ref[...]


def _ada_call(cc, ada_w, ada_b):
    depth, d, n6 = ada_w.shape
    rows = cc.shape[0]
    tn = 1024
    return pl.pallas_call(
        _ada_body,
        grid=(depth, n6 // tn),
        in_specs=[pl.BlockSpec((rows, d), lambda l, j: (0, 0)),
                  pl.BlockSpec((None, d, tn), lambda l, j: (l, 0, j)),
                  pl.BlockSpec((None, 1, tn), lambda l, j: (l, 0, j))],
        out_specs=pl.BlockSpec((None, rows, tn), lambda l, j: (l, 0, j)),
        out_shape=jax.ShapeDtypeStruct((depth, rows, n6), F32),
        compiler_params=_params(("parallel", "parallel")),
        name="ada_mod",
    )(cc, ada_w, ada_b.reshape(depth, 1, n6))


def _rope(xh, cos, ssin, hi):
    sw = jnp.where(hi, pltpu.roll(xh, 32, 1), pltpu.roll(xh, LANES - 32, 1))
    return xh * cos + sw * ssin


def _inproj_body(x_ref, mod_ref, g_ref, w_ref, cos_ref, sin_ref,
                 q_ref, k_ref, v_ref, u_ref, z_ref, ma_ref, mr_ref, *, d, n_ctx):
    x = x_ref[...]
    tt = x.shape[0]
    is_ctx = pl.program_id(1) * tt + lax.broadcasted_iota(I32, (tt, 1), 0) < n_ctx
    shift = jnp.where(is_ctx, mod_ref[0, 0:1, :], mod_ref[1, 0:1, :])
    scale = jnp.where(is_ctx, mod_ref[0, 1:2, :], mod_ref[1, 1:2, :])
    h = _rmsnorm(x, g_ref[...]) * (1.0 + scale) + shift
    hb = h.astype(BF16)
    cos = cos_ref[...]
    ssin = sin_ref[...]
    cos_q = cos * QK_SCALE
    ssin_q = ssin * QK_SCALE
    hi = (lax.broadcasted_iota(I32, cos.shape, 1) & 32) != 0
    qw = N_HEADS * HEAD_DIM
    kw = N_KV_HEADS * HEAD_DIM

    def proj(lo, width):
        return jnp.dot(hb, w_ref[:, lo:lo + width], preferred_element_type=F32)

    for h0 in range(0, N_HEADS, 2):
        qq = proj(h0 * HEAD_DIM, 2 * HEAD_DIM)
        for j in range(2):
            q_ref[:, (h0 + j) * HEAD_DIM:(h0 + j + 1) * HEAD_DIM] = _rope(
                qq[:, j * HEAD_DIM:(j + 1) * HEAD_DIM], cos_q, ssin_q, hi).astype(BF16)
    for h0 in range(0, N_KV_HEADS, 2):
        kk = proj(qw + h0 * HEAD_DIM, 2 * HEAD_DIM)
        for j in range(2):
            k_ref[:, (h0 + j) * HEAD_DIM:(h0 + j + 1) * HEAD_DIM] = _rope(
                kk[:, j * HEAD_DIM:(j + 1) * HEAD_DIM], cos, ssin, hi).astype(BF16)
    off = qw + kw
    v_ref[...] = proj(off, kw).astype(BF16)
    off += kw
    u_ref[...] = proj(off, d)
    off += d
    for c0 in range(0, d, 2 * LANES):
        z_ref[:, c0:c0 + 2 * LANES] = _gelu_tanh(proj(off + c0, 2 * LANES)).astype(BF16)
    off += d
    for c0 in range(0, d, 2 * LANES):
        ma_ref[:, c0:c0 + 2 * LANES] = _sigmoid(proj(off + c0, 2 * LANES)).astype(BF16)
    off += d
    for c0 in range(0, d, 2 * LANES):
        mr_ref[:, c0:c0 + 2 * LANES] = _sigmoid(proj(off + c0, 2 * LANES)).astype(BF16)


def _inproj_call(l, xc, modcat, g, w_in_bf, cos_t, sin_t, *, n_ctx):
    b, t, d = xc.shape
    in_w = w_in_bf.shape[2]
    tt = max(c for c in range(16, INPROJ_MAX_ROWS + 1, 16) if t % c == 0)
    qw = N_HEADS * HEAD_DIM
    kw = N_KV_HEADS * HEAD_DIM
    row = lambda bi, ti: (bi, ti, 0)
    outs = [jax.ShapeDtypeStruct((b, t, qw), BF16), jax.ShapeDtypeStruct((b, t, kw), BF16),
            jax.ShapeDtypeStruct((b, t, kw), BF16), jax.ShapeDtypeStruct((b, t, d), F32),
            jax.ShapeDtypeStruct((b, t, d), BF16), jax.ShapeDtypeStruct((b, t, d), BF16),
            jax.ShapeDtypeStruct((b, t, d), BF16)]
    return pl.pallas_call(
        functools.partial(_inproj_body, d=d, n_ctx=n_ctx),
        grid=(b, t // tt),
        in_specs=[pl.BlockSpec((None, tt, d), row),
                  pl.BlockSpec((None, None, 2, N_MOD, d), lambda bi, ti: (l, bi, 0, 0, 0)),
                  pl.BlockSpec((None, 1, d), lambda bi, ti: (l, 0, 0)),
                  pl.BlockSpec((None, d, in_w), lambda bi, ti: (l, 0, 0)),
                  pl.BlockSpec((tt, HEAD_DIM), lambda bi, ti: (ti, 0)),
                  pl.BlockSpec((tt, HEAD_DIM), lambda bi, ti: (ti, 0))],
        out_specs=[pl.BlockSpec((None, tt, s.shape[2]), row) for s in outs],
        out_shape=outs,
        compiler_params=_params(("parallel", "parallel")),
        name="in_proj",
    )(xc, modcat, g, w_in_bf, cos_t, sin_t)


def _attn_body(q_ref, k_ref, v_ref, sink_ref, o_ref, *, n_ctx, t_all, blk_off, qb):
    nc = n_ctx // ATT_BLOCK
    band = 3 * ATT_BLOCK
    nt = (((1,), (1,)), ((), ()))
    tn = (((0,), (0,)), ((), ()))
    for j in range(qb):
        n = pl.program_id(1) * qb + j + blk_off
        qrows = slice(j * ATT_BLOCK, (j + 1) * ATT_BLOCK)
        start = jnp.clip(n_ctx + (n - nc - 1) * ATT_BLOCK, 0, t_all - band)
        start = pl.multiple_of(start, ATT_BLOCK)

        kpos = start - n_ctx + lax.broadcasted_iota(I32, (band, ATT_BLOCK), 0)
        qpos = (n - nc) * ATT_BLOCK + lax.broadcasted_iota(I32, (band, ATT_BLOCK), 1)
        valid = (kpos >= 0) & (jnp.abs(qpos - kpos) <= ATT_BLOCK) & (n >= nc)
        bias = jnp.where(valid, 0.0, NEG_BIG)
        bias = jnp.concatenate([bias] * GROUP, axis=1)

        for kh in range(N_KV_HEADS):
            lanes = slice(kh * HEAD_DIM, (kh + 1) * HEAD_DIM)
            qs = jnp.concatenate(
                [q_ref[qrows, (kh * GROUP + g) * HEAD_DIM:(kh * GROUP + g + 1) * HEAD_DIM] for g in range(GROUP)],
                axis=0)
            kb = k_ref[pl.ds(start, band), lanes]
            vb = v_ref[pl.ds(start, band), lanes]
            kc = k_ref[0:n_ctx, lanes]
            vc = v_ref[0:n_ctx, lanes]
            s_band = lax.dot_general(kb, qs, nt, preferred_element_type=F32) + bias
            s_ctx = lax.dot_general(kc, qs, nt, preferred_element_type=F32)
            sink = jnp.concatenate(
                [sink_ref[kh * GROUP + g:kh * GROUP + g + 1, :] for g in range(GROUP)], axis=1) * LOG2E
            m = jnp.maximum(sink, jnp.maximum(jnp.max(s_band, axis=0, keepdims=True),
                                              jnp.max(s_ctx, axis=0, keepdims=True)))
            p_band = jnp.exp2(s_band - m)
            p_ctx = jnp.exp2(s_ctx - m)
            denom = (jnp.exp2(sink - m) + jnp.sum(p_band, axis=0, keepdims=True)
                     + jnp.sum(p_ctx, axis=0, keepdims=True))
            o_t = (lax.dot_general(vb, p_band.astype(BF16), tn, preferred_element_type=F32)
                   + lax.dot_general(vc, p_ctx.astype(BF16), tn, preferred_element_type=F32)) / denom
            for g in range(GROUP):
                hh = kh * GROUP + g
                o_ref[qrows, hh * HEAD_DIM:(hh + 1) * HEAD_DIM] = (
                    o_t[:, g * ATT_BLOCK:(g + 1) * ATT_BLOCK].T.astype(BF16))


def _attn_call(l, q, k, v, sink_rows, *, n_ctx, with_ctx):
    b, t, qw = q.shape
    kw = k.shape[2]
    blk_off = 0 if with_ctx else n_ctx // ATT_BLOCK
    nblk = t // ATT_BLOCK - blk_off
    qb = ATT_BLOCKS_PER_STEP if nblk % ATT_BLOCKS_PER_STEP == 0 and blk_off % ATT_BLOCKS_PER_STEP == 0 else 1
    rows = qb * ATT_BLOCK
    return pl.pallas_call(
        functools.partial(_attn_body, n_ctx=n_ctx, t_all=t, blk_off=blk_off, qb=qb),
        grid=(b, nblk // qb),
        in_specs=[pl.BlockSpec((None, rows, qw), lambda bi, ni: (bi, ni + blk_off // qb, 0)),
                  pl.BlockSpec((None, t, kw), lambda bi, ni: (bi, 0, 0)),
                  pl.BlockSpec((None, t, kw), lambda bi, ni: (bi, 0, 0)),
                  pl.BlockSpec((None, N_HEADS, LANES), lambda bi, ni: (l, 0, 0))],
        out_specs=pl.BlockSpec((None, rows, qw), lambda bi, ni: (bi, ni, 0)),
        out_shape=jax.ShapeDtypeStruct((b, nblk * ATT_BLOCK, qw), BF16),
        compiler_params=_params(("parallel", "arbitrary")),
        name="win_attn",
    )(q, k, v, sink_rows)


def _lru_body(u_ref, cw_ref, cb_ref, w_ref, b_ref, lam_ref, y_ref,
              upad, a_f, b_f, a_b, b_b, h_f, h_b, *, n_ctx, n_lat, chunk):
    t_all = n_ctx + n_lat
    pad = SUBLANES
    zeros_pad = jnp.zeros((pad, LANES), F32)
    upad[0:pad, :] = zeros_pad
    upad[pad:pad + n_ctx, :] = u_ref[0:n_ctx, :]
    upad[pad + n_ctx:2 * pad + n_ctx, :] = zeros_pad
    upad[2 * pad + n_ctx:2 * pad + t_all, :] = u_ref[n_ctx:t_all, :]
    upad[2 * pad + t_all:3 * pad + t_all, :] = zeros_pad

    lam = lam_ref[...]
    nl = -lam
    sp = jnp.maximum(nl, 0.0) + jnp.log(1.0 + jnp.exp(-jnp.abs(nl)))
    c2 = sp * (-0.5 * LRU_C * LOG2E)
    cw = cw_ref[...]
    cb = cb_ref[...]
    wcat = w_ref[...]
    bcat = b_ref[...]

    def gate_chunk(ci, carry):
        r0 = pl.multiple_of(ci * chunk, chunk)
        in_lat = r0 >= n_ctx
        p0 = r0 + pad + jnp.where(in_lat, pad, 0)
        win = upad[pl.ds(pl.multiple_of(p0 - pad, pad), chunk + 2 * pad), :]
        uc = cb
        for kk in range(CONV_W):
            lo = pad + kk - CONV_LEFT
            uc = uc + cw[kk:kk + 1, :] * win[lo:lo + chunk, :]
        th = jnp.tanh(jnp.dot(uc.astype(BF16), wcat, preferred_element_type=F32) + bcat)
        uh = 0.5 * uc
        dest_b = pl.multiple_of(jnp.where(in_lat, r0 - n_ctx, r0 + n_lat), chunk)
        for dr, (a_ref, bb_ref, dest) in enumerate(((a_f, b_f, r0), (a_b, b_b, dest_b))):
            t_r = th[:, (2 * dr) * LANES:(2 * dr + 1) * LANES]
            t_i = th[:, (2 * dr + 1) * LANES:(2 * dr + 2) * LANES]
            cc = c2[dr:dr + 1, :]
            a = jnp.exp2(t_r * cc + cc)
            om = 1.0 - a * a
            root = jnp.where(om > 0.0, om * lax.rsqrt(om), 0.0)
            a_ref[pl.ds(dest, chunk), :] = a
            bb_ref[pl.ds(dest, chunk), :] = root * (t_i * uh + uh)
        return carry

    lax.fori_loop(0, t_all // chunk, gate_chunk, 0)

    nseg = SCAN_SEGMENTS
    seg = t_all // nseg

    def rows(i):
        return pl.ds(i, nseg, stride=seg)

    ones = jnp.ones((nseg, LANES), F32)
    zeros = jnp.zeros((nseg, LANES), F32)

    def pass1(i, carry):
        hf, pf, hb, pb = carry
        j = seg - 1 - i
        af = a_f[rows(i), :]
        hf = af * hf + b_f[rows(i), :]
        pf = af * pf
        h_f[rows(i), :] = hf
        ab = a_b[rows(j), :]
        hb = ab * hb + b_b[rows(j), :]
        pb = ab * pb
        h_b[rows(j), :] = hb
        return hf, pf, hb, pb

    hf_fin, pf_tot, hb_fin, pb_tot = lax.fori_loop(0, seg, pass1, (zeros, ones, zeros, ones), unroll=SCAN_UNROLL)

    hin = jnp.zeros((1, LANES), F32)
    hin_f = []
    for s in range(nseg):
        hin_f.append(hin)
        hin = hf_fin[s:s + 1, :] + pf_tot[s:s + 1, :] * hin
    hin = jnp.zeros((1, LANES), F32)
    hin_b = [None] * nseg
    for s in range(nseg - 1, -1, -1):
        hin_b[s] = hin
        hin = hb_fin[s:s + 1, :] + pb_tot[s:s + 1, :] * hin
    hin_f = jnp.concatenate(hin_f, axis=0)
    hin_b = jnp.concatenate(hin_b, axis=0)

    def pass2(i, carry):
        pf, pb = carry
        j = seg - 1 - i
        pf = a_f[rows(i), :] * pf
        h_f[rows(i), :] = h_f[rows(i), :] + pf * hin_f
        pb = a_b[rows(j), :] * pb
        h_b[rows(j), :] = h_b[rows(j), :] + pb * hin_b
        return pf, pb

    lax.fori_loop(0, seg, pass2, (ones, ones), unroll=SCAN_UNROLL)

    y_ref[0:n_ctx, :] = (h_f[0:n_ctx, :] + h_b[n_lat:t_all, :]).astype(BF16)
    y_ref[n_ctx:t_all, :] = (h_f[n_ctx:t_all, :] + h_b[0:n_lat, :]).astype(BF16)


def _lru_call(l, u, conv_w, conv_b, wcat, bcat, lam, *, n_ctx):
    b, t, w = u.shape
    n_lat = t - n_ctx
    chunk = math.gcd(math.gcd(n_ctx, n_lat), 256)
    assert t % SCAN_SEGMENTS == 0 and w == LRU_BLOCKS * LANES
    scr = [pltpu.VMEM((t + 3 * SUBLANES, LANES), F32)] + [pltpu.VMEM((t, LANES), F32) for _ in range(6)]
    return pl.pallas_call(
        functools.partial(_lru_body, n_ctx=n_ctx, n_lat=n_lat, chunk=chunk),
        grid=(b, LRU_BLOCKS),
        in_specs=[pl.BlockSpec((None, t, LANES), lambda bi, ci: (bi, 0, ci)),
                  pl.BlockSpec((None, CONV_W, LANES), lambda bi, ci: (l, 0, ci)),
                  pl.BlockSpec((None, 1, LANES), lambda bi, ci: (l, 0, ci)),
                  pl.BlockSpec((None, None, LANES, 4 * LANES), lambda bi, ci: (l, ci, 0, 0)),
                  pl.BlockSpec((None, None, 1, 4 * LANES), lambda bi, ci: (l, ci, 0, 0)),
                  pl.BlockSpec((None, 2, LANES), lambda bi, ci: (l, 0, ci))],
        out_specs=pl.BlockSpec((None, t, LANES), lambda bi, ci: (bi, 0, ci)),
        out_shape=jax.ShapeDtypeStruct((b, t, w), BF16),
        scratch_shapes=scr,
        compiler_params=_params(("parallel", "parallel")),
        name="rg_lru",
    )(u, conv_w, conv_b, wcat, bcat, lam)


def _merge_body(x_ref, att_ref, y_ref, gz_ref, sa_ref, sr_ref, mod_ref, g_ref,
                wab_ref, wrb_ref, wo_ref, wrh_ref, wrl_ref, xo_ref, h2_ref, aff_ref, mix_s):
    tt, d = x_ref.shape
    ch = 2 * LANES
    n_exp = aff_ref.shape[0]
    att = att_ref[...]
    rec_in = y_ref[...] * gz_ref[...]
    for c0 in range(0, d, ch):
        cols = slice(c0, c0 + ch)
        att_d = jnp.dot(att, wab_ref[:, cols], preferred_element_type=F32)
        rec_d = jnp.dot(rec_in, wrb_ref[:, cols], preferred_element_type=F32)
        mix = sa_ref[:, cols].astype(F32) * att_d + sr_ref[:, cols].astype(F32) * rec_d
        mix_s[:, cols] = mix.astype(BF16)
    mix = mix_s[...]
    part = jnp.zeros((tt, LANES), F32)
    for c0 in range(0, d, ch):
        cols = slice(c0, c0 + ch)
        out = jnp.dot(mix, wo_ref[:, cols], preferred_element_type=F32)
        xn = x_ref[:, cols] + mod_ref[2:3, cols] * out
        xo_ref[:, cols] = xn
        sq = xn * xn
        part = part + sq[:, 0:LANES] + sq[:, LANES:ch]
    rs = lax.rsqrt(jnp.sum(part, axis=-1, keepdims=True) * (1.0 / d) + EPS)
    logits = jnp.zeros((tt, LANES), F32)
    for c0 in range(0, d, ch):
        cols = slice(c0, c0 + ch)
        h2 = xo_ref[:, cols] * rs * g_ref[:, cols] * (1.0 + mod_ref[4:5, cols]) + mod_ref[3:4, cols]
        hi = h2.astype(BF16)
        h2_ref[:, cols] = hi
        lo = (h2 - hi.astype(F32)).astype(BF16)
        logits = (logits + jnp.dot(hi, wrh_ref[cols, :], preferred_element_type=F32)
                  + jnp.dot(lo, wrh_ref[cols, :], preferred_element_type=F32)
                  + jnp.dot(hi, wrl_ref[cols, :], preferred_element_type=F32))
    lt = logits.T[0:n_exp, :]
    mx = jnp.max(lt, axis=0, keepdims=True)
    ex = jnp.exp(lt - mx)
    aff_ref[...] = ex / jnp.sum(ex, axis=0, keepdims=True)


def _merge_call(l, xc, att, y, gz, sa, sr, modcat, g, wab, wrb, wo, wr_hi, wr_lo, *, n_ctx, tt, with_ctx):
    b, t, d = xc.shape
    nct = n_ctx // tt
    off = 0 if with_ctx else nct
    t_out = t - off * tt
    row = lambda bi, ti: (bi, ti + off, 0)
    orow = lambda bi, ti: (bi, ti, 0)
    lay = lambda bi, ti: (l, 0, 0)
    return pl.pallas_call(
        _merge_body,
        grid=(b, t // tt - off),
        in_specs=[pl.BlockSpec((None, tt, d), row), pl.BlockSpec((None, tt, d), orow)]
        + [pl.BlockSpec((None, tt, d), row)] * 4 + [
            pl.BlockSpec((None, None, None, N_MOD, d),
                         lambda bi, ti: (l, bi, jnp.where(ti + off >= nct, 1, 0), 0, 0)),
            pl.BlockSpec((None, 1, d), lay),
            pl.BlockSpec((None,) + wab.shape[1:], lay), pl.BlockSpec((None,) + wrb.shape[1:], lay),
            pl.BlockSpec((None,) + wo.shape[1:], lay),
            pl.BlockSpec((None, d, LANES), lay), pl.BlockSpec((None, d, LANES), lay)],
        out_specs=[pl.BlockSpec((None, tt, d), orow), pl.BlockSpec((None, tt, d), orow),
                   pl.BlockSpec((None, N_EXPERTS, tt), lambda bi, ti: (bi, 0, ti))],
        out_shape=[jax.ShapeDtypeStruct((b, t_out, d), F32), jax.ShapeDtypeStruct((b, t_out, d), BF16),
                   jax.ShapeDtypeStruct((b, N_EXPERTS, t_out), F32)],
        scratch_shapes=[pltpu.VMEM((tt, d), BF16)],
        compiler_params=_params(("parallel", "parallel")),
        name="merge_router",
    )(xc, att, y, gz, sa, sr, modcat, g, wab, wrb, wo, wr_hi, wr_lo)


def _prefix_excl(mask, tri):
    rows, length = mask.shape
    run = jnp.zeros((rows, 1), F32)
    parts = []
    for kb in range(length // LANES):
        blk = mask[:, kb * LANES:(kb + 1) * LANES]
        loc = jnp.dot(blk.astype(BF16), tri, preferred_element_type=F32)
        parts.append(loc + run)
        run = run + jnp.sum(blk, axis=1, keepdims=True)
    return jnp.concatenate(parts, axis=1)


def _count_ge(aff, cand):
    return jnp.sum(jnp.where(aff >= cand, 1.0, 0.0), axis=1, keepdims=True)


def _select_topcap(segments, tri):
    n = len(segments)
    affs = [s[0] for s in segments]
    caps = [float(s[1]) for s in segments]
    rows = affs[0].shape[0]
    base = jnp.full((rows, 1), MIN_NORMAL, F32)
    has = [_count_ge(affs[i], base) >= caps[i] for i in range(n)]
    lo = [base] * n
    for j in range(6, -1, -1):
        f = float(2.0 ** (2 ** j))
        for i in range(n):
            cand = lo[i] * f
            lo[i] = jnp.where(_count_ge(affs[i], cand) >= caps[i], cand, lo[i])
    lo = [jnp.where(has[i], lo[i], 0.0) for i in range(n)]
    cur = list(lo)
    for j in range(1, MANTISSA_STEPS + 1):
        for i in range(n):
            cand = cur[i] + lo[i] * float(2.0 ** -j)
            cur[i] = jnp.where(_count_ge(affs[i], cand) >= caps[i], cand, cur[i])
    codes = []
    for i in range(n):
        aff = affs[i]
        nxt = jnp.where(has[i], cur[i] + lo[i] * float(2.0 ** -23), base)
        ge = jnp.where(aff >= cur[i], 1.0, 0.0)
        tie = ge * jnp.where(aff < nxt, 1.0, 0.0)
        excess = jnp.sum(ge, axis=1, keepdims=True) - caps[i]
        after = jnp.sum(tie, axis=1, keepdims=True) - (_prefix_excl(tie, tri) + tie)
        sel = ge - tie * jnp.where(after < excess, 1.0, 0.0)
        pos = _prefix_excl(sel, tri)
        codes.append(jnp.where(sel > 0.5, pos, -1.0))
    return codes


def _route_body(aff_ref, code_ref, affrow_ref, bnd_ref, *, n_ctx, n_lat, tt):
    aff = aff_ref[...]
    e = aff.shape[0]
    r = lax.broadcasted_iota(I32, (LANES, LANES), 0)
    c = lax.broadcasted_iota(I32, (LANES, LANES), 1)
    tri = (r < c).astype(BF16)
    cap_lat = EC_FACTOR * n_lat // N_EXPERTS
    if n_ctx:
        cap_ctx = EC_FACTOR * n_ctx // N_EXPERTS
        code_ctx, code = _select_topcap([(aff[:, 0:n_ctx], cap_ctx), (aff[:, n_ctx:n_ctx + n_lat], cap_lat)], tri)
        code = jnp.concatenate([code_ctx, jnp.where(code >= 0.0, code + cap_ctx, -1.0)], axis=1)
    else:
        (code,) = _select_topcap([(aff, cap_lat)], tri)
    taken = jnp.where(code >= 0.0, 1.0, 0.0)
    run = jnp.zeros((e, 1), F32)
    bnds = [run]
    for t0 in range(0, n_ctx + n_lat, tt):
        run = run + jnp.sum(taken[:, t0:t0 + tt], axis=1, keepdims=True)
        bnds.append(run)
    bnd_ref[...] = jnp.concatenate(bnds, axis=1).astype(I32)
    code = code.astype(I32)
    for ei in range(e):
        code_ref[ei] = code[ei:ei + 1, :]
        affrow_ref[ei] = aff[ei:ei + 1, :]


def _route_call(aff, *, n_ctx, tt):
    b, e, t = aff.shape
    spec4 = pl.BlockSpec((None, e, 1, t), lambda bi: (bi, 0, 0, 0))
    nb = t // tt + 1
    return pl.pallas_call(
        functools.partial(_route_body, n_ctx=n_ctx, n_lat=t - n_ctx, tt=tt),
        grid=(b,),
        in_specs=[pl.BlockSpec((None, e, t), lambda bi: (bi, 0, 0))],
        out_specs=[spec4, spec4, pl.BlockSpec((None, e, nb), lambda bi: (bi, 0, 0))],
        out_shape=[jax.ShapeDtypeStruct((b, e, 1, t), I32), jax.ShapeDtypeStruct((b, e, 1, t), F32),
                   jax.ShapeDtypeStruct((b, e, nb), I32)],
        compiler_params=_params(("parallel",)),
        name="route_select",
    )(aff)


def _moe_body(code_ref, aff_ref, h_ref, wg32_ref, wu32_ref, wd32_ref, ye_ref, wg_s, wu_s, wd_s,
              *, n_ctx, n_ctx_slots):
    p = pl.program_id(0)
    bi = pl.program_id(1)
    n_exp = pl.num_programs(0) - 1
    fill = p % 2
    use = 1 - fill

    @pl.when(p < n_exp)
    def _():
        rd = wg32_ref.shape[0]
        rf = wd32_ref.shape[0]
        r0 = pl.multiple_of(bi * rd, rd)
        f0 = pl.multiple_of(bi * rf, rf)
        wg_s[fill, pl.ds(r0, rd), :] = wg32_ref[...].astype(BF16)
        wu_s[fill, pl.ds(r0, rd), :] = wu32_ref[...].astype(BF16)
        wd_s[fill, pl.ds(f0, rf), :] = wd32_ref[...].astype(BF16)

    @pl.when(p == 0)
    def _():
        ye_ref[...] = jnp.zeros(ye_ref.shape, ye_ref.dtype)

    @pl.when(p > 0)
    def _():
        n_slots = ye_ref.shape[0]
        t = code_ref.shape[1]
        xs, gs = [], []
        for (s0, s1, t0, t1) in ((0, n_ctx_slots, 0, n_ctx), (n_ctx_slots, n_slots, n_ctx, t)):
            if s1 == s0:
                continue
            hit = code_ref[:, t0:t1] == s0 + lax.broadcasted_iota(I32, (s1 - s0, t1 - t0), 0)
            sel = jnp.where(hit, 1.0, 0.0).astype(BF16)
            xs.append(jnp.dot(sel, h_ref[t0:t1, :], preferred_element_type=F32).astype(BF16))
            gs.append(jnp.sum(jnp.where(hit, aff_ref[:, t0:t1], 0.0), axis=1, keepdims=True))
        xe = jnp.concatenate(xs, axis=0)
        gcol = jnp.concatenate(gs, axis=0)
        gate = jnp.dot(xe, wg_s[use], preferred_element_type=F32)
        up = jnp.dot(xe, wu_s[use], preferred_element_type=F32)
        hid = (gate * _sigmoid(gate) * up).astype(BF16)
        ye = jnp.dot(hid, wd_s[use], preferred_element_type=F32) * gcol
        ye_ref[...] = ye.astype(BF16)


def _moe_call(l, code, aff4, h2, wg, wu, wd, *, n_slots, n_ctx):
    b, t, d = h2.shape
    _, e, _, f = wg.shape
    assert d % (b * 16) == 0 and f % (b * 16) == 0
    rd, rf = d // b, f // b
    n_ctx_slots = EC_FACTOR * n_ctx // N_EXPERTS
    ex = lambda pi: jnp.maximum(pi - 1, 0)
    wmap = lambda pi, bi: (l, jnp.minimum(pi, e - 1), jnp.where(pi < e, bi, b - 1), 0)
    return pl.pallas_call(
        functools.partial(_moe_body, n_ctx=n_ctx, n_ctx_slots=n_ctx_slots),
        grid=(e + 1, b),
        in_specs=[pl.BlockSpec((None, None, 1, t), lambda pi, bi: (bi, ex(pi), 0, 0)),
                  pl.BlockSpec((None, None, 1, t), lambda pi, bi: (bi, ex(pi), 0, 0)),
                  pl.BlockSpec((None, t, d), lambda pi, bi: (bi, 0, 0)),
                  pl.BlockSpec((None, None, rd, f), wmap),
                  pl.BlockSpec((None, None, rd, f), wmap),
                  pl.BlockSpec((None, None, rf, d), wmap)],
        out_specs=pl.BlockSpec((None, None, n_slots, d), lambda pi, bi: (bi, jnp.where(pi == 0, e, pi - 1), 0, 0)),
        out_shape=jax.ShapeDtypeStruct((b, e + 1, n_slots, d), BF16),
        scratch_shapes=[pltpu.VMEM((2, d, f), BF16), pltpu.VMEM((2, d, f), BF16), pltpu.VMEM((2, f, d), BF16)],
        compiler_params=_params(("arbitrary", "arbitrary")),
        name="moe_ffn",
    )(code, aff4, h2, wg, wu, wd)


def _combine_body(bnd_ref, x_ref, code_ref, ye_ref, mod_ref, gf_ref, o_ref, *, n_slots, final, win):
    bi = pl.program_id(0)
    ti = pl.program_id(1)
    nb = pl.num_programs(1) + 1
    e = ye_ref.shape[0]
    tt, d = x_ref.shape
    tn = (((0,), (0,)), ((), ()))
    starts = []
    fits = None
    for ei in range(e):
        at = (bi * e + ei) * nb + ti
        lo = bnd_ref[at]
        hi = bnd_ref[at + 1]
        st = jnp.minimum(lax.shift_left(lax.shift_right_logical(lo, 4), 4), n_slots - win)
        starts.append(pl.multiple_of(st, 16))
        ok = hi - st <= win
        fits = ok if fits is None else jnp.logical_and(fits, ok)

    def finish(acc):
        xn = x_ref[...] + mod_ref[5:6, :] * acc
        if final:
            xn = _rmsnorm(xn, gf_ref[...])
        o_ref[...] = xn

    @pl.when(fits)
    def _():
        slot = lax.broadcasted_iota(I32, (win, tt), 0)
        sel = jnp.concatenate(
            [jnp.where(code_ref[ei] - starts[ei] == slot, 1.0, 0.0).astype(BF16) for ei in range(e)], axis=0)
        ye = jnp.concatenate([ye_ref[ei, pl.ds(starts[ei], win), :] for ei in range(e)], axis=0)
        finish(lax.dot_general(sel, ye, tn, preferred_element_type=F32))

    @pl.when(jnp.logical_not(fits))
    def _():
        slot = lax.broadcasted_iota(I32, (n_slots, tt), 0)
        sel = jnp.concatenate(
            [jnp.where(code_ref[ei] == slot, 1.0, 0.0).astype(BF16) for ei in range(e)], axis=0)
        ye = ye_ref[...].reshape(e * n_slots, d)
        finish(lax.dot_general(sel, ye, tn, preferred_element_type=F32))


def _combine_call(l, xc, code, bounds, ye, modcat, gfinal, *, n_ctx, tt, final):
    b, t, d = xc.shape
    e, n_slots = N_EXPERTS, ye.shape[2]
    nct = n_ctx // tt
    win = min(COMBINE_WINDOW, n_slots)
    grid_spec = pltpu.PrefetchScalarGridSpec(
        num_scalar_prefetch=1,
        grid=(b, t // tt),
        in_specs=[pl.BlockSpec((None, tt, d), lambda bi, ti, bnd: (bi, ti, 0)),
                  pl.BlockSpec((None, e, 1, tt), lambda bi, ti, bnd: (bi, 0, 0, ti)),
                  pl.BlockSpec((None, e, n_slots, d), lambda bi, ti, bnd: (bi, 0, 0, 0)),
                  pl.BlockSpec((None, None, None, N_MOD, d),
                               lambda bi, ti, bnd: (l, bi, jnp.where(ti >= nct, 1, 0), 0, 0)),
                  pl.BlockSpec((1, d), lambda bi, ti, bnd: (0, 0))],
        out_specs=pl.BlockSpec((None, tt, d), lambda bi, ti, bnd: (bi, ti, 0)))
    return pl.pallas_call(
        functools.partial(_combine_body, n_slots=n_slots, final=final, win=win),
        grid_spec=grid_spec,
        out_shape=jax.ShapeDtypeStruct((b, t, d), F32),
        compiler_params=_params(("parallel", "arbitrary")),
        name="moe_combine",
    )(bounds.reshape(-1), xc, code, ye, modcat, gfinal)


def _rope_tables(n_ctx, n_lat):
    quarter = HEAD_DIM // 4
    t = jnp.arange(n_lat, dtype=jnp.int32)
    rows = (t // GRID_W).astype(F32)
    cols = (t % GRID_W).astype(F32)
    freqs = ROPE_THETA ** (-jnp.arange(quarter, dtype=F32) / quarter)
    ang_r = rows[:, None] * freqs[None, :]
    ang_c = cols[:, None] * freqs[None, :]
    cos = jnp.concatenate([jnp.cos(ang_r), jnp.cos(ang_r), jnp.cos(ang_c), jnp.cos(ang_c)], axis=1)
    sin = jnp.concatenate([-jnp.sin(ang_r), jnp.sin(ang_r), -jnp.sin(ang_c), jnp.sin(ang_c)], axis=1)
    cos = jnp.concatenate([jnp.ones((n_ctx, HEAD_DIM), F32), cos], axis=0)
    sin = jnp.concatenate([jnp.zeros((n_ctx, HEAD_DIM), F32), sin], axis=0)
    return cos, sin


def kernel(x, c, ctx, c_ctx, ada_w, ada_b, norm_mix_g, w_in, attn_sink, conv_w, conv_b, lru_wa, lru_ba, lru_wx,
           lru_bx, lru_lambda, w_attn_br, w_rec_br, w_out, norm_ffn_g, w_router, w_gate, w_up, w_down,
           final_norm_g):
    b, n_lat, d = x.shape
    n_ctx = ctx.shape[1]
    depth = ada_w.shape[0]
    tt = math.gcd(math.gcd(n_ctx, n_lat), 256)

    pad_rows = (-(b + 1)) % SUBLANES
    cc = jnp.concatenate([c, c_ctx[None, :], jnp.zeros((pad_rows, d), F32)], axis=0)
    mod = _ada_call(cc, ada_w, ada_b)
    mod_lat = mod[:, :b].reshape(depth, b, 1, N_MOD, d)
    mod_ctx = jnp.broadcast_to(mod[:, b].reshape(depth, 1, 1, N_MOD, d), (depth, b, 1, N_MOD, d))
    modcat = jnp.concatenate([mod_ctx, mod_lat], axis=2)

    cos_t, sin_t = _rope_tables(n_ctx, n_lat)
    xc = jnp.concatenate([ctx, x], axis=1)

    w_in_bf = w_in.astype(BF16)
    wab_bf = w_attn_br.astype(BF16)
    wrb_bf = w_rec_br.astype(BF16)
    wo_bf = w_out.astype(BF16)
    wcat = (0.5 * jnp.concatenate([lru_wa[:, 0], lru_wx[:, 0], lru_wa[:, 1], lru_wx[:, 1]], axis=-1)).astype(BF16)
    bcat = 0.5 * jnp.stack([lru_ba[:, 0], lru_bx[:, 0], lru_ba[:, 1], lru_bx[:, 1]], axis=1)
    bcat = bcat.reshape(depth, 4, LRU_BLOCKS, LANES).transpose(0, 2, 1, 3).reshape(depth, LRU_BLOCKS, 1, 4 * LANES)
    sink_rows = jnp.broadcast_to(attn_sink[:, :, None], (depth, N_HEADS, LANES))
    wr_pad = jnp.pad(w_router, ((0, 0), (0, 0), (0, LANES - w_router.shape[2])))
    wr_hi = wr_pad.astype(BF16)
    wr_lo = (wr_pad - wr_hi.astype(F32)).astype(BF16)
    g_mix = norm_mix_g[:, None, :]
    g_ffn = norm_ffn_g[:, None, :]
    conv_b3 = conv_b[:, None, :]

    cap_lat = EC_FACTOR * n_lat // N_EXPERTS
    cap_ctx = EC_FACTOR * n_ctx // N_EXPERTS
    for l in range(depth):
        last = l == depth - 1
        with_ctx = not last
        q, k, v, u, gz, sa, sr = _inproj_call(l, xc, modcat, g_mix, w_in_bf, cos_t, sin_t, n_ctx=n_ctx)
        att = _attn_call(l, q, k, v, sink_rows, n_ctx=n_ctx, with_ctx=with_ctx)
        y = _lru_call(l, u, conv_w, conv_b3, wcat, bcat, lru_lambda, n_ctx=n_ctx)
        xc, h2, aff = _merge_call(l, xc, att, y, gz, sa, sr, modcat, g_ffn, wab_bf, wrb_bf, wo_bf, wr_hi, wr_lo,
                                  n_ctx=n_ctx, tt=tt, with_ctx=with_ctx)
        ctx_rows = n_ctx if with_ctx else 0
        code, aff_rows, bounds = _route_call(aff, n_ctx=ctx_rows, tt=tt)
        n_slots = cap_lat + (cap_ctx if with_ctx else 0)
        ye = _moe_call(l, code, aff_rows, h2, w_gate, w_up, w_down, n_slots=n_slots, n_ctx=ctx_rows)
        xc = _combine_call(l, xc, code, bounds, ye, modcat, final_norm_g[None, :], n_ctx=ctx_rows, tt=tt, final=last)
    return xc
```

```python
import functools
import math

import jax
import jax.numpy as jnp
from jax import lax
from jax.experimental import pallas as pl
from jax.experimental.pallas import tpu as pltpu

F32 = jnp.float32
BF16 = jnp.bfloat16
I32 = jnp.int32

HEAD_DIM = 128
N_HEADS = 8
N_KV_HEADS = 2
GROUP = N_HEADS // N_KV_HEADS
ATT_BLOCK = 128
GRID_W = 64
ROPE_THETA = 10000.0
LRU_BLOCKS = 8
LRU_C = 8.0
CONV_W = 4
CONV_LEFT = 2
N_EXPERTS = 16
EC_FACTOR = 2
N_MOD = 6
EPS = 1e-6
NEG_BIG = -1e30
LOG2E = 1.4426950408889634
QK_SCALE = HEAD_DIM ** -0.5 * LOG2E

LANES = 128
SUBLANES = 8
SCAN_SEGMENTS = 64
GATE_UNROLL = 3
INPROJ_MAX_ROWS = 768
ATT_BLOCKS_PER_STEP = 2
COMBINE_WINDOW = 96
MIN_NORMAL = 2.0 ** -126
MANTISSA_STEPS = 32
VMEM_LIMIT = 56 * 1024 * 1024


def _sigmoid(x):
    return 0.5 * jnp.tanh(0.5 * x) + 0.5


def _gelu_tanh(x):
    c = math.sqrt(2.0 / math.pi)
    return 0.5 * x * (1.0 + jnp.tanh(c * (x + 0.044715 * (x * x * x))))


def _rmsnorm(x, g):
    ms = jnp.mean(x * x, axis=-1, keepdims=True)
    return x * lax.rsqrt(ms + EPS) * g


def _params(sem):
    return pltpu.CompilerParams(dimension_semantics=sem, vmem_limit_bytes=VMEM_LIMIT)


def _ada_body(c_ref, w_ref, b_ref, o_ref):
    c = c_ref[...]
    act = c * _sigmoid(c)
    o_ref[...] = jnp.dot(act, w_ref[...], preferred_element_type=F32,
                         precision=lax.Precision.HIGHEST) + b_ref[...]


def _ada_call(cc, ada_w, ada_b):
    depth, d, n6 = ada_w.shape
    rows = cc.shape[0]
    tn = 1024
    return pl.pallas_call(
        _ada_body,
        grid=(depth, n6 // tn),
        in_specs=[pl.BlockSpec((rows, d), lambda l, j: (0, 0)),
                  pl.BlockSpec((None, d, tn), lambda l, j: (l, 0, j)),
                  pl.BlockSpec((None, 1, tn), lambda l, j: (l, 0, j))],
        out_specs=pl.BlockSpec((None, rows, tn), lambda l, j: (l, 0, j)),
        out_shape=jax.ShapeDtypeStruct((depth, rows, n6), F32),
        compiler_params=_params(("parallel", "parallel")),
        name="ada_mod",
    )(cc, ada_w, ada_b.reshape(depth, 1, n6))


def _rope(xh, cos, ssin, hi):
    sw = jnp.where(hi, pltpu.roll(xh, 32, 1), pltpu.roll(xh, LANES - 32, 1))
    return xh * cos + sw * ssin


def _inproj_body(x_ref, mod_ref, g_ref, w_ref, cos_ref, sin_ref,
                 q_ref, k_ref, v_ref, u_ref, z_ref, ma_ref, mr_ref, *, d, n_ctx):
    x = x_ref[...]
    tt = x.shape[0]
    is_ctx = pl.program_id(1) * tt + lax.broadcasted_iota(I32, (tt, 1), 0) < n_ctx
    shift = jnp.where(is_ctx, mod_ref[0, 0:1, :], mod_ref[1, 0:1, :])
    scale = jnp.where(is_ctx, mod_ref[0, 1:2, :], mod_ref[1, 1:2, :])
    h = _rmsnorm(x, g_ref[...]) * (1.0 + scale) + shift
    hb = h.astype(BF16)
    cos = cos_ref[...]
    ssin = sin_ref[...]
    cos_q = cos * QK_SCALE
    ssin_q = ssin * QK_SCALE
    hi = (lax.broadcasted_iota(I32, cos.shape, 1) & 32) != 0
    qw = N_HEADS * HEAD_DIM
    kw = N_KV_HEADS * HEAD_DIM

    def proj(lo, width):
        return jnp.dot(hb, w_ref[:, lo:lo + width], preferred_element_type=F32)

    for h0 in range(0, N_HEADS, 2):
        qq = proj(h0 * HEAD_DIM, 2 * HEAD_DIM)
        for j in range(2):
            q_ref[:, (h0 + j) * HEAD_DIM:(h0 + j + 1) * HEAD_DIM] = _rope(
                qq[:, j * HEAD_DIM:(j + 1) * HEAD_DIM], cos_q, ssin_q, hi).astype(BF16)
    for h0 in range(0, N_KV_HEADS, 2):
        kk = proj(qw + h0 * HEAD_DIM, 2 * HEAD_DIM)
        for j in range(2):
            k_ref[:, (h0 + j) * HEAD_DIM:(h0 + j + 1) * HEAD_DIM] = _rope(
                kk[:, j * HEAD_DIM:(j + 1) * HEAD_DIM], cos, ssin, hi).astype(BF16)
    off = qw + kw
    v_ref[...] = proj(off, kw).astype(BF16)
    off += kw
    u_ref[...] = proj(off, d)
    off += d
    for c0 in range(0, d, 2 * LANES):
        z_ref[:, c0:c0 + 2 * LANES] = _gelu_tanh(proj(off + c0, 2 * LANES)).astype(BF16)
    off += d
    for c0 in range(0, d, 2 * LANES):
        ma_ref[:, c0:c0 + 2 * LANES] = _sigmoid(proj(off + c0, 2 * LANES)).astype(BF16)
    off += d
    for c0 in range(0, d, 2 * LANES):
        mr_ref[:, c0:c0 + 2 * LANES] = _sigmoid(proj(off + c0, 2 * LANES)).astype(BF16)


def _inproj_call(l, xc, modcat, g, w_in_bf, cos_t, sin_t, *, n_ctx):
    b, t, d = xc.shape
    in_w = w_in_bf.shape[2]
    tt = max(c for c in range(16, INPROJ_MAX_ROWS + 1, 16) if t % c == 0)
    qw = N_HEADS * HEAD_DIM
    kw = N_KV_HEADS * HEAD_DIM
    row = lambda bi, ti: (bi, ti, 0)
    outs = [jax.ShapeDtypeStruct((b, t, qw), BF16), jax.ShapeDtypeStruct((b, t, kw), BF16),
            jax.ShapeDtypeStruct((b, t, kw), BF16), jax.ShapeDtypeStruct((b, t, d), F32),
            jax.ShapeDtypeStruct((b, t, d), BF16), jax.ShapeDtypeStruct((b, t, d), BF16),
            jax.ShapeDtypeStruct((b, t, d), BF16)]
    return pl.pallas_call(
        functools.partial(_inproj_body, d=d, n_ctx=n_ctx),
        grid=(b, t // tt),
        in_specs=[pl.BlockSpec((None, tt, d), row),
                  pl.BlockSpec((None, None, 2, N_MOD, d), lambda bi, ti: (l, bi, 0, 0, 0)),
                  pl.BlockSpec((None, 1, d), lambda bi, ti: (l, 0, 0)),
                  pl.BlockSpec((None, d, in_w), lambda bi, ti: (l, 0, 0)),
                  pl.BlockSpec((tt, HEAD_DIM), lambda bi, ti: (ti, 0)),
                  pl.BlockSpec((tt, HEAD_DIM), lambda bi, ti: (ti, 0))],
        out_specs=[pl.BlockSpec((None, tt, s.shape[2]), row) for s in outs],
        out_shape=outs,
        compiler_params=_params(("parallel", "parallel")),
        name="in_proj",
    )(xc, modcat, g, w_in_bf, cos_t, sin_t)


def _attn_body(q_ref, k_ref, v_ref, sink_ref, o_ref, *, n_ctx, t_all, blk_off, qb):
    nc = n_ctx // ATT_BLOCK
    band = 3 * ATT_BLOCK
    nt = (((1,), (1,)), ((), ()))
    tn = (((0,), (0,)), ((), ()))
    for j in range(qb):
        n = pl.program_id(1) * qb + j + blk_off
        qrows = slice(j * ATT_BLOCK, (j + 1) * ATT_BLOCK)
        start = jnp.clip(n_ctx + (n - nc - 1) * ATT_BLOCK, 0, t_all - band)
        start = pl.multiple_of(start, ATT_BLOCK)

        kpos = start - n_ctx + lax.broadcasted_iota(I32, (band, ATT_BLOCK), 0)
        qpos = (n - nc) * ATT_BLOCK + lax.broadcasted_iota(I32, (band, ATT_BLOCK), 1)
        valid = (kpos >= 0) & (jnp.abs(qpos - kpos) <= ATT_BLOCK) & (n >= nc)
        bias = jnp.where(valid, 0.0, NEG_BIG)
        bias = jnp.concatenate([bias] * GROUP, axis=1)

        for kh in range(N_KV_HEADS):
            lanes = slice(kh * HEAD_DIM, (kh + 1) * HEAD_DIM)
            qs = jnp.concatenate(
                [q_ref[qrows, (kh * GROUP + g) * HEAD_DIM:(kh * GROUP + g + 1) * HEAD_DIM] for g in range(GROUP)],
                axis=0)
            kb = k_ref[pl.ds(start, band), lanes]
            vb = v_ref[pl.ds(start, band), lanes]
            kc = k_ref[0:n_ctx, lanes]
            vc = v_ref[0:n_ctx, lanes]
            s_band = lax.dot_general(kb, qs, nt, preferred_element_type=F32) + bias
            s_ctx = lax.dot_general(kc, qs, nt, preferred_element_type=F32)
            sink = jnp.concatenate(
                [sink_ref[kh * GROUP + g:kh * GROUP + g + 1, :] for g in range(GROUP)], axis=1) * LOG2E
            m = jnp.maximum(sink, jnp.maximum(jnp.max(s_band, axis=0, keepdims=True),
                                              jnp.max(s_ctx, axis=0, keepdims=True)))
            p_band = jnp.exp2(s_band - m)
            p_ctx = jnp.exp2(s_ctx - m)
            denom = (jnp.exp2(sink - m) + jnp.sum(p_band, axis=0, keepdims=True)
                     + jnp.sum(p_ctx, axis=0, keepdims=True))
            o_t = (lax.dot_general(vb, p_band.astype(BF16), tn, preferred_element_type=F32)
                   + lax.dot_general(vc, p_ctx.astype(BF16), tn, preferred_element_type=F32)) / denom
            for g in range(GROUP):
                hh = kh * GROUP + g
                o_ref[qrows, hh * HEAD_DIM:(hh + 1) * HEAD_DIM] = (
                    o_t[:, g * ATT_BLOCK:(g + 1) * ATT_BLOCK].T.astype(BF16))


def _attn_call(l, q, k, v, sink_rows, *, n_ctx, with_ctx):
    b, t, qw = q.shape
    kw = k.shape[2]
    blk_off = 0 if with_ctx else n_ctx // ATT_BLOCK
    nblk = t // ATT_BLOCK - blk_off
    qb = ATT_BLOCKS_PER_STEP if nblk % ATT_BLOCKS_PER_STEP == 0 and blk_off % ATT_BLOCKS_PER_STEP == 0 else 1
    rows = qb * ATT_BLOCK
    return pl.pallas_call(
        functools.partial(_attn_body, n_ctx=n_ctx, t_all=t, blk_off=blk_off, qb=qb),
        grid=(b, nblk // qb),
        in_specs=[pl.BlockSpec((None, rows, qw), lambda bi, ni: (bi, ni + blk_off // qb, 0)),
                  pl.BlockSpec((None, t, kw), lambda bi, ni: (bi, 0, 0)),
                  pl.BlockSpec((None, t, kw), lambda bi, ni: (bi, 0, 0)),
                  pl.BlockSpec((None, N_HEADS, LANES), lambda bi, ni: (l, 0, 0))],
        out_specs=pl.BlockSpec((None, rows, qw), lambda bi, ni: (bi, ni, 0)),
        out_shape=jax.ShapeDtypeStruct((b, nblk * ATT_BLOCK, qw), BF16),
        compiler_params=_params(("parallel", "arbitrary")),
        name="win_attn",
    )(q, k, v, sink_rows)


def _lru_body(u_ref, cw_ref, cb_ref, w_ref, lam_ref, y_ref,
              upad, a_f, a_b, b_b, h_f, h_b, y32, *, n_ctx, n_lat):
    t_all = n_ctx + n_lat
    pad = SUBLANES
    nseg = SCAN_SEGMENTS
    seg = t_all // nseg
    zeros_pad = jnp.zeros((pad, LANES), F32)
    upad[0:pad, :] = zeros_pad
    upad[pad:pad + t_all, :] = u_ref[...]
    upad[pad + t_all:2 * pad + t_all, :] = zeros_pad

    lam = lam_ref[...]
    nl = -lam
    sp = jnp.maximum(nl, 0.0) + jnp.log(1.0 + jnp.exp(-jnp.abs(nl)))
    c2 = sp * (-0.5 * LRU_C * LOG2E)
    cw = cw_ref[...]
    cb = cb_ref[...]
    wcat = w_ref[...]
    lane = lax.broadcasted_iota(I32, (SCAN_SEGMENTS, LANES), 1)
    bias_cols = jnp.where(lane < 2, 1.0, 0.0).astype(BF16)

    def rows(i):
        return pl.ds(i, nseg, stride=seg)

    seg_id = lax.broadcasted_iota(I32, (nseg, LANES), 0)
    ones = jnp.ones((nseg, LANES), F32)
    zeros = jnp.zeros((nseg, LANES), F32)

    def same_part(r, q):
        return (r < n_ctx) == (q < n_ctx)

    def tap(i, off):
        v = upad[pl.ds(pad + i + off, nseg, stride=seg), :]
        for s in range(nseg):
            r = i + seg * s
            if 0 <= r + off < t_all and not same_part(r, r + off):
                v = jnp.where(seg_id == s, 0.0, v)
        return v

    brk_s, brk_i = (n_ctx - 1) // seg, (n_ctx - 1) % seg

    hf_fin, pf_tot = zeros, ones
    for i in range(seg):
        uc = cb
        for kk in range(CONV_W):
            uc = uc + cw[kk:kk + 1, :] * tap(i, kk - CONV_LEFT)
        lhs = jnp.concatenate([uc.astype(BF16), bias_cols], axis=1)
        th = jnp.tanh(jnp.dot(lhs, wcat, preferred_element_type=F32))
        uh = 0.5 * uc
        ab_dir = []
        for dr in range(2):
            t_r = th[:, (2 * dr) * LANES:(2 * dr + 1) * LANES]
            t_i = th[:, (2 * dr + 1) * LANES:(2 * dr + 2) * LANES]
            cc = c2[dr:dr + 1, :]
            a = jnp.exp2(t_r * cc + cc)
            om = 1.0 - a * a
            root = jnp.where(om > 0.0, om * lax.rsqrt(om), 0.0)
            ab_dir.append((a, root * (t_i * uh + uh)))
        af, bf = ab_dir[0]
        hf_fin = af * hf_fin + bf
        pf_tot = af * pf_tot
        a_f[rows(i), :] = af
        h_f[rows(i), :] = hf_fin
        ab, bb = ab_dir[1]
        if i == brk_i:
            ab = jnp.where(seg_id == brk_s, 0.0, ab)
        a_b[rows(i), :] = ab
        b_b[rows(i), :] = bb

    hin = jnp.zeros((1, LANES), F32)
    hin_f = []
    for s in range(nseg):
        hin_f.append(hin)
        hin = hf_fin[s:s + 1, :] + pf_tot[s:s + 1, :] * hin
    hin_f = jnp.concatenate(hin_f, axis=0)

    hb_fin, pb_tot, pf = zeros, ones, ones
    for i in range(seg):
        j = seg - 1 - i
        ab = a_b[rows(j), :]
        hb_fin = ab * hb_fin + b_b[rows(j), :]
        pb_tot = ab * pb_tot
        h_b[rows(j), :] = hb_fin
        pf = a_f[rows(i), :] * pf
        h_f[rows(i), :] = h_f[rows(i), :] + pf * hin_f

    order = [(brk_s - k) % nseg for k in range(nseg)]
    hin_b = [None] * nseg
    hin = jnp.zeros((1, LANES), F32)
    for s in order:
        hin_b[s] = hin
        hin = hb_fin[s:s + 1, :] + pb_tot[s:s + 1, :] * hin
    hin_b[brk_s] = hin
    hin_b = jnp.concatenate(hin_b, axis=0)

    pb = ones
    for i in range(seg):
        j = seg - 1 - i
        pb = a_b[rows(j), :] * pb
        y32[rows(j), :] = h_f[rows(j), :] + (h_b[rows(j), :] + pb * hin_b)
    y_ref[...] = y32[...].astype(BF16)


def _lru_call(l, u, conv_w, conv_b, wcat, lam, *, n_ctx):
    b, t, w = u.shape
    n_lat = t - n_ctx
    assert t % SCAN_SEGMENTS == 0 and w == LRU_BLOCKS * LANES and n_ctx > 0
    scr = [pltpu.VMEM((t + 2 * SUBLANES, LANES), F32)] + [pltpu.VMEM((t, LANES), F32) for _ in range(6)]
    return pl.pallas_call(
        functools.partial(_lru_body, n_ctx=n_ctx, n_lat=n_lat),
        grid=(b, LRU_BLOCKS),
        in_specs=[pl.BlockSpec((None, t, LANES), lambda bi, ci: (bi, 0, ci)),
                  pl.BlockSpec((None, CONV_W, LANES), lambda bi, ci: (l, 0, ci)),
                  pl.BlockSpec((None, 1, LANES), lambda bi, ci: (l, 0, ci)),
                  pl.BlockSpec((None, None, 2 * LANES, 4 * LANES), lambda bi, ci: (l, ci, 0, 0)),
                  pl.BlockSpec((None, 2, LANES), lambda bi, ci: (l, 0, ci))],
        out_specs=pl.BlockSpec((None, t, LANES), lambda bi, ci: (bi, 0, ci)),
        out_shape=jax.ShapeDtypeStruct((b, t, w), BF16),
        scratch_shapes=scr,
        compiler_params=_params(("parallel", "parallel")),
        name="rg_lru",
    )(u, conv_w, conv_b, wcat, lam)


def _merge_body(x_ref, att_ref, y_ref, gz_ref, sa_ref, sr_ref, mod_ref, g_ref,
                wab_ref, wrb_ref, wo_ref, wrh_ref, wrl_ref, xo_ref, h2_ref, aff_ref, mix_s):
    tt, d = x_ref.shape
    ch = 2 * LANES
    n_exp = aff_ref.shape[0]
    att = att_ref[...]
    rec_in = y_ref[...] * gz_ref[...]
    for c0 in range(0, d, ch):
        cols = slice(c0, c0 + ch)
        att_d = jnp.dot(att, wab_ref[:, cols], preferred_element_type=F32)
        rec_d = jnp.dot(rec_in, wrb_ref[:, cols], preferred_element_type=F32)
        mix = sa_ref[:, cols].astype(F32) * att_d + sr_ref[:, cols].astype(F32) * rec_d
        mix_s[:, cols] = mix.astype(BF16)
    mix = mix_s[...]
    part = jnp.zeros((tt, LANES), F32)
    for c0 in range(0, d, ch):
        cols = slice(c0, c0 + ch)
        out = jnp.dot(mix, wo_ref[:, cols], preferred_element_type=F32)
        xn = x_ref[:, cols] + mod_ref[2:3, cols] * out
        xo_ref[:, cols] = xn
        sq = xn * xn
        part = part + sq[:, 0:LANES] + sq[:, LANES:ch]
    rs = lax.rsqrt(jnp.sum(part, axis=-1, keepdims=True) * (1.0 / d) + EPS)
    logits = jnp.zeros((tt, LANES), F32)
    for c0 in range(0, d, ch):
        cols = slice(c0, c0 + ch)
        h2 = xo_ref[:, cols] * rs * g_ref[:, cols] * (1.0 + mod_ref[4:5, cols]) + mod_ref[3:4, cols]
        hi = h2.astype(BF16)
        h2_ref[:, cols] = hi
        lo = (h2 - hi.astype(F32)).astype(BF16)
        logits = (logits + jnp.dot(hi, wrh_ref[cols, :], preferred_element_type=F32)
                  + jnp.dot(lo, wrh_ref[cols, :], preferred_element_type=F32)
                  + jnp.dot(hi, wrl_ref[cols, :], preferred_element_type=F32))
    lt = logits.T[0:n_exp, :]
    mx = jnp.max(lt, axis=0, keepdims=True)
    ex = jnp.exp(lt - mx)
    aff_ref[...] = ex / jnp.sum(ex, axis=0, keepdims=True)


def _merge_call(l, xc, att, y, gz, sa, sr, modcat, g, wab, wrb, wo, wr_hi, wr_lo, *, n_ctx, tt, with_ctx):
    b, t, d = xc.shape
    nct = n_ctx // tt
    off = 0 if with_ctx else nct
    t_out = t - off * tt
    row = lambda bi, ti: (bi, ti + off, 0)
    orow = lambda bi, ti: (bi, ti, 0)
    lay = lambda bi, ti: (l, 0, 0)
    return pl.pallas_call(
        _merge_body,
        grid=(b, t // tt - off),
        in_specs=[pl.BlockSpec((None, tt, d), row), pl.BlockSpec((None, tt, d), orow)]
        + [pl.BlockSpec((None, tt, d), row)] * 4 + [
            pl.BlockSpec((None, None, None, N_MOD, d),
                         lambda bi, ti: (l, bi, jnp.where(ti + off >= nct, 1, 0), 0, 0)),
            pl.BlockSpec((None, 1, d), lay),
            pl.BlockSpec((None,) + wab.shape[1:], lay), pl.BlockSpec((None,) + wrb.shape[1:], lay),
            pl.BlockSpec((None,) + wo.shape[1:], lay),
            pl.BlockSpec((None, d, LANES), lay), pl.BlockSpec((None, d, LANES), lay)],
        out_specs=[pl.BlockSpec((None, tt, d), orow), pl.BlockSpec((None, tt, d), orow),
                   pl.BlockSpec((None, N_EXPERTS, tt), lambda bi, ti: (bi, 0, ti))],
        out_shape=[jax.ShapeDtypeStruct((b, t_out, d), F32), jax.ShapeDtypeStruct((b, t_out, d), BF16),
                   jax.ShapeDtypeStruct((b, N_EXPERTS, t_out), F32)],
        scratch_shapes=[pltpu.VMEM((tt, d), BF16)],
        compiler_params=_params(("parallel", "parallel")),
        name="merge_router",
    )(xc, att, y, gz, sa, sr, modcat, g, wab, wrb, wo, wr_hi, wr_lo)


def _prefix_excl(mask, tri):
    rows, length = mask.shape
    run = jnp.zeros((rows, 1), F32)
    parts = []
    for kb in range(length // LANES):
        blk = mask[:, kb * LANES:(kb + 1) * LANES]
        loc = jnp.dot(blk.astype(BF16), tri, preferred_element_type=F32)
        parts.append(loc + run)
        run = run + jnp.sum(blk, axis=1, keepdims=True)
    return jnp.concatenate(parts, axis=1)


def _count_ge(aff, cand):
    return jnp.sum(jnp.where(aff >= cand, 1.0, 0.0), axis=1, keepdims=True)


def _select_topcap(segments, tri):
    n = len(segments)
    affs = [s[0] for s in segments]
    caps = [float(s[1]) for s in segments]
    rows = affs[0].shape[0]
    base = jnp.full((rows, 1), MIN_NORMAL, F32)
    has = [_count_ge(affs[i], base) >= caps[i] for i in range(n)]
    lo = [base] * n
    for j in range(6, -1, -1):
        f = float(2.0 ** (2 ** j))
        for i in range(n):
            cand = lo[i] * f
            lo[i] = jnp.where(_count_ge(affs[i], cand) >= caps[i], cand, lo[i])
    lo = [jnp.where(has[i], lo[i], 0.0) for i in range(n)]
    cur = list(lo)
    for j in range(1, MANTISSA_STEPS + 1):
        for i in range(n):
            cand = cur[i] + lo[i] * float(2.0 ** -j)
            cur[i] = jnp.where(_count_ge(affs[i], cand) >= caps[i], cand, cur[i])
    codes = []
    for i in range(n):
        aff = affs[i]
        nxt = jnp.where(has[i], cur[i] + lo[i] * float(2.0 ** -23), base)
        ge = jnp.where(aff >= cur[i], 1.0, 0.0)
        tie = ge * jnp.where(aff < nxt, 1.0, 0.0)
        excess = jnp.sum(ge, axis=1, keepdims=True) - caps[i]
        after = jnp.sum(tie, axis=1, keepdims=True) - (_prefix_excl(tie, tri) + tie)
        sel = ge - tie * jnp.where(after < excess, 1.0, 0.0)
        pos = _prefix_excl(sel, tri)
        codes.append(jnp.where(sel > 0.5, pos, -1.0))
    return codes


def _route_body(aff_ref, code_ref, affrow_ref, bnd_ref, *, n_ctx, n_lat, tt):
    aff = aff_ref[...]
    e = aff.shape[0]
    r = lax.broadcasted_iota(I32, (LANES, LANES), 0)
    c = lax.broadcasted_iota(I32, (LANES, LANES), 1)
    tri = (r < c).astype(BF16)
    cap_lat = EC_FACTOR * n_lat // N_EXPERTS
    if n_ctx:
        cap_ctx = EC_FACTOR * n_ctx // N_EXPERTS
        code_ctx, code = _select_topcap([(aff[:, 0:n_ctx], cap_ctx), (aff[:, n_ctx:n_ctx + n_lat], cap_lat)], tri)
        code = jnp.concatenate([code_ctx, jnp.where(code >= 0.0, code + cap_ctx, -1.0)], axis=1)
    else:
        (code,) = _select_topcap([(aff, cap_lat)], tri)
    taken = jnp.where(code >= 0.0, 1.0, 0.0)
    run = jnp.zeros((e, 1), F32)
    bnds = [run]
    for t0 in range(0, n_ctx + n_lat, tt):
        run = run + jnp.sum(taken[:, t0:t0 + tt], axis=1, keepdims=True)
        bnds.append(run)
    bnd_ref[...] = jnp.concatenate(bnds, axis=1).astype(I32)
    code = code.astype(I32)
    for ei in range(e):
        code_ref[ei] = code[ei:ei + 1, :]
        affrow_ref[ei] = aff[ei:ei + 1, :]


def _route_call(aff, *, n_ctx, tt):
    b, e, t = aff.shape
    spec4 = pl.BlockSpec((None, e, 1, t), lambda bi: (bi, 0, 0, 0))
    nb = t // tt + 1
    return pl.pallas_call(
        functools.partial(_route_body, n_ctx=n_ctx, n_lat=t - n_ctx, tt=tt),
        grid=(b,),
        in_specs=[pl.BlockSpec((None, e, t), lambda bi: (bi, 0, 0))],
        out_specs=[spec4, spec4, pl.BlockSpec((None, e, nb), lambda bi: (bi, 0, 0))],
        out_shape=[jax.ShapeDtypeStruct((b, e, 1, t), I32), jax.ShapeDtypeStruct((b, e, 1, t), F32),
                   jax.ShapeDtypeStruct((b, e, nb), I32)],
        compiler_params=_params(("parallel",)),
        name="route_select",
    )(aff)


def _moe_body(code_ref, aff_ref, h_ref, wg32_ref, wu32_ref, wd32_ref, ye_ref, wg_s, wu_s, wd_s,
              *, n_ctx, n_ctx_slots):
    p = pl.program_id(0)
    bi = pl.program_id(1)
    n_exp = pl.num_programs(0) - 1
    fill = p % 2
    use = 1 - fill

    @pl.when(p < n_exp)
    def _():
        rd = wg32_ref.shape[0]
        rf = wd32_ref.shape[0]
        r0 = pl.multiple_of(bi * rd, rd)
        f0 = pl.multiple_of(bi * rf, rf)
        wg_s[fill, pl.ds(r0, rd), :] = wg32_ref[...].astype(BF16)
        wu_s[fill, pl.ds(r0, rd), :] = wu32_ref[...].astype(BF16)
        wd_s[fill, pl.ds(f0, rf), :] = wd32_ref[...].astype(BF16)

    @pl.when(p == 0)
    def _():
        ye_ref[...] = jnp.zeros(ye_ref.shape, ye_ref.dtype)

    @pl.when(p > 0)
    def _():
        n_slots = ye_ref.shape[0]
        t = code_ref.shape[1]
        xs, gs = [], []
        for (s0, s1, t0, t1) in ((0, n_ctx_slots, 0, n_ctx), (n_ctx_slots, n_slots, n_ctx, t)):
            if s1 == s0:
                continue
            hit = code_ref[:, t0:t1] == s0 + lax.broadcasted_iota(I32, (s1 - s0, t1 - t0), 0)
            sel = jnp.where(hit, 1.0, 0.0).astype(BF16)
            xs.append(jnp.dot(sel, h_ref[t0:t1, :], preferred_element_type=F32).astype(BF16))
            gs.append(jnp.sum(jnp.where(hit, aff_ref[:, t0:t1], 0.0), axis=1, keepdims=True))
        xe = jnp.concatenate(xs, axis=0)
        gcol = jnp.concatenate(gs, axis=0)
        gate = jnp.dot(xe, wg_s[use], preferred_element_type=F32)
        up = jnp.dot(xe, wu_s[use], preferred_element_type=F32)
        hid = (gate * _sigmoid(gate) * up).astype(BF16)
        ye = jnp.dot(hid, wd_s[use], preferred_element_type=F32) * gcol
        ye_ref[...] = ye.astype(BF16)


def _moe_call(l, code, aff4, h2, wg, wu, wd, *, n_slots, n_ctx):
    b, t, d = h2.shape
    _, e, _, f = wg.shape
    assert d % (b * 16) == 0 and f % (b * 16) == 0
    rd, rf = d // b, f // b
    n_ctx_slots = EC_FACTOR * n_ctx // N_EXPERTS
    ex = lambda pi: jnp.maximum(pi - 1, 0)
    wmap = lambda pi, bi: (l, jnp.minimum(pi, e - 1), jnp.where(pi < e, bi, b - 1), 0)
    return pl.pallas_call(
        functools.partial(_moe_body, n_ctx=n_ctx, n_ctx_slots=n_ctx_slots),
        grid=(e + 1, b),
        in_specs=[pl.BlockSpec((None, None, 1, t), lambda pi, bi: (bi, ex(pi), 0, 0)),
                  pl.BlockSpec((None, None, 1, t), lambda pi, bi: (bi, ex(pi), 0, 0)),
                  pl.BlockSpec((None, t, d), lambda pi, bi: (bi, 0, 0)),
                  pl.BlockSpec((None, None, rd, f), wmap),
                  pl.BlockSpec((None, None, rd, f), wmap),
                  pl.BlockSpec((None, None, rf, d), wmap)],
        out_specs=pl.BlockSpec((None, None, n_slots, d), lambda pi, bi: (bi, jnp.where(pi == 0, e, pi - 1), 0, 0)),
        out_shape=jax.ShapeDtypeStruct((b, e + 1, n_slots, d), BF16),
        scratch_shapes=[pltpu.VMEM((2, d, f), BF16), pltpu.VMEM((2, d, f), BF16), pltpu.VMEM((2, f, d), BF16)],
        compiler_params=_params(("arbitrary", "arbitrary")),
        name="moe_ffn",
    )(code, aff4, h2, wg, wu, wd)


def _combine_body(bnd_ref, x_ref, code_ref, ye_ref, mod_ref, gf_ref, o_ref, *, n_slots, final, win):
    bi = pl.program_id(0)
    ti = pl.program_id(1)
    nb = pl.num_programs(1) + 1
    e = ye_ref.shape[0]
    tt, d = x_ref.shape
    tn = (((0,), (0,)), ((), ()))
    starts = []
    fits = None
    for ei in range(e):
        at = (bi * e + ei) * nb + ti
        lo = bnd_ref[at]
        hi = bnd_ref[at + 1]
        st = jnp.minimum(lax.shift_left(lax.shift_right_logical(lo, 4), 4), n_slots - win)
        starts.append(pl.multiple_of(st, 16))
        ok = hi - st <= win
        fits = ok if fits is None else jnp.logical_and(fits, ok)

    def finish(acc):
        xn = x_ref[...] + mod_ref[5:6, :] * acc
        if final:
            xn = _rmsnorm(xn, gf_ref[...])
        o_ref[...] = xn

    @pl.when(fits)
    def _():
        slot = lax.broadcasted_iota(I32, (win, tt), 0)
        sel = jnp.concatenate(
            [jnp.where(code_ref[ei] - starts[ei] == slot, 1.0, 0.0).astype(BF16) for ei in range(e)], axis=0)
        ye = jnp.concatenate([ye_ref[ei, pl.ds(starts[ei], win), :] for ei in range(e)], axis=0)
        finish(lax.dot_general(sel, ye, tn, preferred_element_type=F32))

    @pl.when(jnp.logical_not(fits))
    def _():
        slot = lax.broadcasted_iota(I32, (n_slots, tt), 0)
        sel = jnp.concatenate(
            [jnp.where(code_ref[ei] == slot, 1.0, 0.0).astype(BF16) for ei in range(e)], axis=0)
        ye = ye_ref[...].reshape(e * n_slots, d)
        finish(lax.dot_general(sel, ye, tn, preferred_element_type=F32))


def _combine_call(l, xc, code, bounds, ye, modcat, gfinal, *, n_ctx, tt, final):
    b, t, d = xc.shape
    e, n_slots = N_EXPERTS, ye.shape[2]
    nct = n_ctx // tt
    win = min(COMBINE_WINDOW, n_slots)
    grid_spec = pltpu.PrefetchScalarGridSpec(
        num_scalar_prefetch=1,
        grid=(b, t // tt),
        in_specs=[pl.BlockSpec((None, tt, d), lambda bi, ti, bnd: (bi, ti, 0)),
                  pl.BlockSpec((None, e, 1, tt), lambda bi, ti, bnd: (bi, 0, 0, ti)),
                  pl.BlockSpec((None, e, n_slots, d), lambda bi, ti, bnd: (bi, 0, 0, 0)),
                  pl.BlockSpec((None, None, None, N_MOD, d),
                               lambda bi, ti, bnd: (l, bi, jnp.where(ti >= nct, 1, 0), 0, 0)),
                  pl.BlockSpec((1, d), lambda bi, ti, bnd: (0, 0))],
        out_specs=pl.BlockSpec((None, tt, d), lambda bi, ti, bnd: (bi, ti, 0)))
    return pl.pallas_call(
        functools.partial(_combine_body, n_slots=n_slots, final=final, win=win),
        grid_spec=grid_spec,
        out_shape=jax.ShapeDtypeStruct((b, t, d), F32),
        compiler_params=_params(("parallel", "arbitrary")),
        name="moe_combine",
    )(bounds.reshape(-1), xc, code, ye, modcat, gfinal)


def _rope_tables(n_ctx, n_lat):
    quarter = HEAD_DIM // 4
    t = jnp.arange(n_lat, dtype=jnp.int32)
    rows = (t // GRID_W).astype(F32)
    cols = (t % GRID_W).astype(F32)
    freqs = ROPE_THETA ** (-jnp.arange(quarter, dtype=F32) / quarter)
    ang_r = rows[:, None] * freqs[None, :]
    ang_c = cols[:, None] * freqs[None, :]
    cos = jnp.concatenate([jnp.cos(ang_r), jnp.cos(ang_r), jnp.cos(ang_c), jnp.cos(ang_c)], axis=1)
    sin = jnp.concatenate([-jnp.sin(ang_r), jnp.sin(ang_r), -jnp.sin(ang_c), jnp.sin(ang_c)], axis=1)
    cos = jnp.concatenate([jnp.ones((n_ctx, HEAD_DIM), F32), cos], axis=0)
    sin = jnp.concatenate([jnp.zeros((n_ctx, HEAD_DIM), F32), sin], axis=0)
    return cos, sin


def kernel(x, c, ctx, c_ctx, ada_w, ada_b, norm_mix_g, w_in, attn_sink, conv_w, conv_b, lru_wa, lru_ba, lru_wx,
           lru_bx, lru_lambda, w_attn_br, w_rec_br, w_out, norm_ffn_g, w_router, w_gate, w_up, w_down,
           final_norm_g):
    b, n_lat, d = x.shape
    n_ctx = ctx.shape[1]
    depth = ada_w.shape[0]
    tt = math.gcd(math.gcd(n_ctx, n_lat), 256)

    pad_rows = (-(b + 1)) % SUBLANES
    cc = jnp.concatenate([c, c_ctx[None, :], jnp.zeros((pad_rows, d), F32)], axis=0)
    mod = _ada_call(cc, ada_w, ada_b)
    mod_lat = mod[:, :b].reshape(depth, b, 1, N_MOD, d)
    mod_ctx = jnp.broadcast_to(mod[:, b].reshape(depth, 1, 1, N_MOD, d), (depth, b, 1, N_MOD, d))
    modcat = jnp.concatenate([mod_ctx, mod_lat], axis=2)

    cos_t, sin_t = _rope_tables(n_ctx, n_lat)
    xc = jnp.concatenate([ctx, x], axis=1)

    w_in_bf = w_in.astype(BF16)
    wab_bf = w_attn_br.astype(BF16)
    wrb_bf = w_rec_br.astype(BF16)
    wo_bf = w_out.astype(BF16)
    wcat = (0.5 * jnp.concatenate([lru_wa[:, 0], lru_wx[:, 0], lru_wa[:, 1], lru_wx[:, 1]], axis=-1)).astype(BF16)
    bcat = 0.5 * jnp.stack([lru_ba[:, 0], lru_bx[:, 0], lru_ba[:, 1], lru_bx[:, 1]], axis=1)
    bcat = bcat.reshape(depth, 4, LRU_BLOCKS, LANES).transpose(0, 2, 1, 3).reshape(depth, LRU_BLOCKS, 1, 4 * LANES)
    b_hi = bcat.astype(BF16)
    b_lo = (bcat - b_hi.astype(F32)).astype(BF16)
    wcat = jnp.concatenate(
        [wcat, b_hi, b_lo, jnp.zeros((depth, LRU_BLOCKS, LANES - 2, 4 * LANES), BF16)], axis=2)
    sink_rows = jnp.broadcast_to(attn_sink[:, :, None], (depth, N_HEADS, LANES))
    wr_pad = jnp.pad(w_router, ((0, 0), (0, 0), (0, LANES - w_router.shape[2])))
    wr_hi = wr_pad.astype(BF16)
    wr_lo = (wr_pad - wr_hi.astype(F32)).astype(BF16)
    g_mix = norm_mix_g[:, None, :]
    g_ffn = norm_ffn_g[:, None, :]
    conv_b3 = conv_b[:, None, :]

    cap_lat = EC_FACTOR * n_lat // N_EXPERTS
    cap_ctx = EC_FACTOR * n_ctx // N_EXPERTS
    for l in range(depth):
        last = l == depth - 1
        with_ctx = not last
        q, k, v, u, gz, sa, sr = _inproj_call(l, xc, modcat, g_mix, w_in_bf, cos_t, sin_t, n_ctx=n_ctx)
        att = _attn_call(l, q, k, v, sink_rows, n_ctx=n_ctx, with_ctx=with_ctx)
        y = _lru_call(l, u, conv_w, conv_b3, wcat, lru_lambda, n_ctx=n_ctx)
        xc, h2, aff = _merge_call(l, xc, att, y, gz, sa, sr, modcat, g_ffn, wab_bf, wrb_bf, wo_bf, wr_hi, wr_lo,
                                  n_ctx=n_ctx, tt=tt, with_ctx=with_ctx)
        ctx_rows = n_ctx if with_ctx else 0
        code, aff_rows, bounds = _route_call(aff, n_ctx=ctx_rows, tt=tt)
        n_slots = cap_lat + (cap_ctx if with_ctx else 0)
        ye = _moe_call(l, code, aff_rows, h2, w_gate, w_up, w_down, n_slots=n_slots, n_ctx=ctx_rows)
        xc = _combine_call(l, xc, code, bounds, ye, modcat, final_norm_g[None, :], n_ctx=ctx_rows, tt=tt, final=last)
    return xc
```

```python
import functools
import math

import jax
import jax.numpy as jnp
from jax import lax
from jax.experimental import pallas as pl
from jax.experimental.pallas import tpu as pltpu

F32 = jnp.float32
BF16 = jnp.bfloat16
I32 = jnp.int32

HEAD_DIM = 128
N_HEADS = 8
N_KV_HEADS = 2
GROUP = N_HEADS // N_KV_HEADS
ATT_BLOCK = 128
GRID_W = 64
ROPE_THETA = 10000.0
LRU_BLOCKS = 8
LRU_C = 8.0
CONV_W = 4
CONV_LEFT = 2
N_EXPERTS = 16
EC_FACTOR = 2
N_MOD = 6
EPS = 1e-6
NEG_BIG = -1e30
LOG2E = 1.4426950408889634
QK_SCALE = HEAD_DIM ** -0.5 * LOG2E

LANES = 128
SUBLANES = 8
SCAN_SEGMENTS = 64
GATE_UNROLL = 3
INPROJ_MAX_ROWS = 768
INPROJ_ROW_PARTS = 2
ATT_BLOCKS_PER_STEP = 2
COMBINE_WINDOW = 80
MIN_NORMAL = 2.0 ** -126
MANTISSA_STEPS = 32
VMEM_LIMIT = 56 * 1024 * 1024


def _sigmoid(x):
    return 0.5 * jnp.tanh(0.5 * x) + 0.5


def _gelu_tanh(x):
    c = math.sqrt(2.0 / math.pi)
    return 0.5 * x * (1.0 + jnp.tanh(c * (x + 0.044715 * (x * x * x))))


def _rmsnorm(x, g):
    ms = jnp.mean(x * x, axis=-1, keepdims=True)
    return x * lax.rsqrt(ms + EPS) * g


def _params(sem):
    return pltpu.CompilerParams(dimension_semantics=sem, vmem_limit_bytes=VMEM_LIMIT)


def _ada_body(c_ref, w_ref, b_ref, o_ref):
    c = c_ref[...]
    act = c * _sigmoid(c)
    o_ref[...] = jnp.dot(act, w_ref[...], preferred_element_type=F32,
                         precision=lax.Precision.HIGHEST) + b_ref[...]


def _ada_call(cc, ada_w, ada_b):
    depth, d, n6 = ada_w.shape
    rows = cc.shape[0]
    tn = 1024
    return pl.pallas_call(
        _ada_body,
        grid=(depth, n6 // tn),
        in_specs=[pl.BlockSpec((rows, d), lambda l, j: (0, 0)),
                  pl.BlockSpec((None, d, tn), lambda l, j: (l, 0, j)),
                  pl.BlockSpec((None, 1, tn), lambda l, j: (l, 0, j))],
        out_specs=pl.BlockSpec((None, rows, tn), lambda l, j: (l, 0, j)),
        out_shape=jax.ShapeDtypeStruct((depth, rows, n6), F32),
        compiler_params=_params(("parallel", "parallel")),
        name="ada_mod",
    )(cc, ada_w, ada_b.reshape(depth, 1, n6))


def _rope(xh, cos, ssin, hi):
    sw = jnp.where(hi, pltpu.roll(xh, 32, 1), pltpu.roll(xh, LANES - 32, 1))
    return xh * cos + sw * ssin


def _inproj_body(x_ref, mod_ref, g_ref, w_ref, cos_ref, sin_ref,
                 q_ref, k_ref, v_ref, u_ref, z_ref, ma_ref, mr_ref, *, d, n_ctx):
    tt = x_ref.shape[0]
    qw = N_HEADS * HEAD_DIM
    kw = N_KV_HEADS * HEAD_DIM
    parts = INPROJ_ROW_PARTS if tt % (16 * INPROJ_ROW_PARTS) == 0 else 1
    tp = tt // parts
    for r0 in range(0, tt, tp):
        rows = slice(r0, r0 + tp)
        x = x_ref[rows, :]
        is_ctx = pl.program_id(1) * tt + r0 + lax.broadcasted_iota(I32, (tp, 1), 0) < n_ctx
        shift = jnp.where(is_ctx, mod_ref[0, 0:1, :], mod_ref[1, 0:1, :])
        scale = jnp.where(is_ctx, mod_ref[0, 1:2, :], mod_ref[1, 1:2, :])
        h = _rmsnorm(x, g_ref[...]) * (1.0 + scale) + shift
        hb = h.astype(BF16)
        cos = cos_ref[rows, :]
        ssin = sin_ref[rows, :]
        cos_q = cos * QK_SCALE
        ssin_q = ssin * QK_SCALE
        hi = (lax.broadcasted_iota(I32, cos.shape, 1) & 32) != 0

        def proj(lo, width, hb=hb):
            return jnp.dot(hb, w_ref[:, lo:lo + width], preferred_element_type=F32)

        for h0 in range(0, N_HEADS, 2):
            qq = proj(h0 * HEAD_DIM, 2 * HEAD_DIM)
            for j in range(2):
                q_ref[rows, (h0 + j) * HEAD_DIM:(h0 + j + 1) * HEAD_DIM] = _rope(
                    qq[:, j * HEAD_DIM:(j + 1) * HEAD_DIM], cos_q, ssin_q, hi).astype(BF16)
        for h0 in range(0, N_KV_HEADS, 2):
            kk = proj(qw + h0 * HEAD_DIM, 2 * HEAD_DIM)
            for j in range(2):
                k_ref[rows, (h0 + j) * HEAD_DIM:(h0 + j + 1) * HEAD_DIM] = _rope(
                    kk[:, j * HEAD_DIM:(j + 1) * HEAD_DIM], cos, ssin, hi).astype(BF16)
        off = qw + kw
        v_ref[rows, :] = proj(off, kw).astype(BF16)
        off += kw
        u_ref[rows, :] = proj(off, d)
        off += d
        for c0 in range(0, d, 2 * LANES):
            z_ref[rows, c0:c0 + 2 * LANES] = _gelu_tanh(proj(off + c0, 2 * LANES)).astype(BF16)
        off += d
        for c0 in range(0, d, 2 * LANES):
            ma_ref[rows, c0:c0 + 2 * LANES] = _sigmoid(proj(off + c0, 2 * LANES)).astype(BF16)
        off += d
        for c0 in range(0, d, 2 * LANES):
            mr_ref[rows, c0:c0 + 2 * LANES] = _sigmoid(proj(off + c0, 2 * LANES)).astype(BF16)


def _inproj_call(l, xc, modcat, g, w_in_bf, cos_t, sin_t, *, n_ctx):
    b, t, d = xc.shape
    in_w = w_in_bf.shape[2]
    tt = max(c for c in range(16, INPROJ_MAX_ROWS + 1, 16) if t % c == 0)
    qw = N_HEADS * HEAD_DIM
    kw = N_KV_HEADS * HEAD_DIM
    row = lambda bi, ti: (bi, ti, 0)
    outs = [jax.ShapeDtypeStruct((b, t, qw), BF16), jax.ShapeDtypeStruct((b, t, kw), BF16),
            jax.ShapeDtypeStruct((b, t, kw), BF16), jax.ShapeDtypeStruct((b, t, d), F32),
            jax.ShapeDtypeStruct((b, t, d), BF16), jax.ShapeDtypeStruct((b, t, d), BF16),
            jax.ShapeDtypeStruct((b, t, d), BF16)]
    return pl.pallas_call(
        functools.partial(_inproj_body, d=d, n_ctx=n_ctx),
        grid=(b, t // tt),
        in_specs=[pl.BlockSpec((None, tt, d), row),
                  pl.BlockSpec((None, None, 2, N_MOD, d), lambda bi, ti: (l, bi, 0, 0, 0)),
                  pl.BlockSpec((None, 1, d), lambda bi, ti: (l, 0, 0)),
                  pl.BlockSpec((None, d, in_w), lambda bi, ti: (l, 0, 0)),
                  pl.BlockSpec((tt, HEAD_DIM), lambda bi, ti: (ti, 0)),
                  pl.BlockSpec((tt, HEAD_DIM), lambda bi, ti: (ti, 0))],
        out_specs=[pl.BlockSpec((None, tt, s.shape[2]), row) for s in outs],
        out_shape=outs,
        compiler_params=_params(("parallel", "parallel")),
        name="in_proj",
    )(xc, modcat, g, w_in_bf, cos_t, sin_t)


def _attn_body(q_ref, k_ref, v_ref, sink_ref, o_ref, *, n_ctx, t_all, blk_off, qb):
    nc = n_ctx // ATT_BLOCK
    band = 3 * ATT_BLOCK
    nt = (((1,), (1,)), ((), ()))
    tn = (((0,), (0,)), ((), ()))
    for j in range(qb):
        n = pl.program_id(1) * qb + j + blk_off
        qrows = slice(j * ATT_BLOCK, (j + 1) * ATT_BLOCK)
        start = jnp.clip(n_ctx + (n - nc - 1) * ATT_BLOCK, 0, t_all - band)
        start = pl.multiple_of(start, ATT_BLOCK)

        kpos = start - n_ctx + lax.broadcasted_iota(I32, (band, ATT_BLOCK), 0)
        qpos = (n - nc) * ATT_BLOCK + lax.broadcasted_iota(I32, (band, ATT_BLOCK), 1)
        valid = (kpos >= 0) & (jnp.abs(qpos - kpos) <= ATT_BLOCK) & (n >= nc)
        bias = jnp.where(valid, 0.0, NEG_BIG)
        bias = jnp.concatenate([bias] * GROUP, axis=1)

        for kh in range(N_KV_HEADS):
            lanes = slice(kh * HEAD_DIM, (kh + 1) * HEAD_DIM)
            qs = jnp.concatenate(
                [q_ref[qrows, (kh * GROUP + g) * HEAD_DIM:(kh * GROUP + g + 1) * HEAD_DIM] for g in range(GROUP)],
                axis=0)
            kb = k_ref[pl.ds(start, band), lanes]
            vb = v_ref[pl.ds(start, band), lanes]
            kc = k_ref[0:n_ctx, lanes]
            vc = v_ref[0:n_ctx, lanes]
            s_band = lax.dot_general(kb, qs, nt, preferred_element_type=F32) + bias
            s_ctx = lax.dot_general(kc, qs, nt, preferred_element_type=F32)
            sink = jnp.concatenate(
                [sink_ref[kh * GROUP + g:kh * GROUP + g + 1, :] for g in range(GROUP)], axis=1) * LOG2E
            m = jnp.maximum(sink, jnp.maximum(jnp.max(s_band, axis=0, keepdims=True),
                                              jnp.max(s_ctx, axis=0, keepdims=True)))
            p_band = jnp.exp2(s_band - m)
            p_ctx = jnp.exp2(s_ctx - m)
            denom = (jnp.exp2(sink - m) + jnp.sum(p_band, axis=0, keepdims=True)
                     + jnp.sum(p_ctx, axis=0, keepdims=True))
            o_t = (lax.dot_general(vb, p_band.astype(BF16), tn, preferred_element_type=F32)
                   + lax.dot_general(vc, p_ctx.astype(BF16), tn, preferred_element_type=F32)) / denom
            for g in range(GROUP):
                hh = kh * GROUP + g
                o_ref[qrows, hh * HEAD_DIM:(hh + 1) * HEAD_DIM] = (
                    o_t[:, g * ATT_BLOCK:(g + 1) * ATT_BLOCK].T.astype(BF16))


def _attn_call(l, q, k, v, sink_rows, *, n_ctx, with_ctx):
    b, t, qw = q.shape
    kw = k.shape[2]
    blk_off = 0 if with_ctx else n_ctx // ATT_BLOCK
    nblk = t // ATT_BLOCK - blk_off
    qb = ATT_BLOCKS_PER_STEP if nblk % ATT_BLOCKS_PER_STEP == 0 and blk_off % ATT_BLOCKS_PER_STEP == 0 else 1
    rows = qb * ATT_BLOCK
    return pl.pallas_call(
        functools.partial(_attn_body, n_ctx=n_ctx, t_all=t, blk_off=blk_off, qb=qb),
        grid=(b, nblk // qb),
        in_specs=[pl.BlockSpec((None, rows, qw), lambda bi, ni: (bi, ni + blk_off // qb, 0)),
                  pl.BlockSpec((None, t, kw), lambda bi, ni: (bi, 0, 0)),
                  pl.BlockSpec((None, t, kw), lambda bi, ni: (bi, 0, 0)),
                  pl.BlockSpec((None, N_HEADS, LANES), lambda bi, ni: (l, 0, 0))],
        out_specs=pl.BlockSpec((None, rows, qw), lambda bi, ni: (bi, ni, 0)),
        out_shape=jax.ShapeDtypeStruct((b, nblk * ATT_BLOCK, qw), BF16),
        compiler_params=_params(("parallel", "arbitrary")),
        name="win_attn",
    )(q, k, v, sink_rows)


def _lru_body(u_ref, cw_ref, cb_ref, w_ref, lam_ref, y_ref,
              upad, a_f, a_b, b_b, h_f, h_b, y32, *, n_ctx, n_lat):
    t_all = n_ctx + n_lat
    pad = SUBLANES
    nseg = SCAN_SEGMENTS
    seg = t_all // nseg
    zeros_pad = jnp.zeros((pad, LANES), F32)
    upad[0:pad, :] = zeros_pad
    upad[pad:pad + t_all, :] = u_ref[...]
    upad[pad + t_all:2 * pad + t_all, :] = zeros_pad

    lam = lam_ref[...]
    nl = -lam
    sp = jnp.maximum(nl, 0.0) + jnp.log(1.0 + jnp.exp(-jnp.abs(nl)))
    c2 = sp * (-0.5 * LRU_C * LOG2E)
    cw = cw_ref[...]
    cb = cb_ref[...]
    wcat = w_ref[...]
    lane = lax.broadcasted_iota(I32, (SCAN_SEGMENTS, LANES), 1)
    bias_cols = jnp.where(lane < 2, 1.0, 0.0).astype(BF16)

    def rows(i):
        return pl.ds(i, nseg, stride=seg)

    seg_id = lax.broadcasted_iota(I32, (nseg, LANES), 0)
    ones = jnp.ones((nseg, LANES), F32)
    zeros = jnp.zeros((nseg, LANES), F32)

    def same_part(r, q):
        return (r < n_ctx) == (q < n_ctx)

    def tap(i, off):
        v = upad[pl.ds(pad + i + off, nseg, stride=seg), :]
        for s in range(nseg):
            r = i + seg * s
            if 0 <= r + off < t_all and not same_part(r, r + off):
                v = jnp.where(seg_id == s, 0.0, v)
        return v

    brk_s, brk_i = (n_ctx - 1) // seg, (n_ctx - 1) % seg

    hf_fin, pf_tot = zeros, ones
    for i in range(seg):
        uc = cb
        for kk in range(CONV_W):
            uc = uc + cw[kk:kk + 1, :] * tap(i, kk - CONV_LEFT)
        lhs = jnp.concatenate([uc.astype(BF16), bias_cols], axis=1)
        th = jnp.tanh(jnp.dot(lhs, wcat, preferred_element_type=F32))
        uh = 0.5 * uc
        ab_dir = []
        for dr in range(2):
            t_r = th[:, (2 * dr) * LANES:(2 * dr + 1) * LANES]
            t_i = th[:, (2 * dr + 1) * LANES:(2 * dr + 2) * LANES]
            cc = c2[dr:dr + 1, :]
            a = jnp.exp2(t_r * cc + cc)
            om = 1.0 - a * a
            root = jnp.where(om > 0.0, om * lax.rsqrt(om), 0.0)
            ab_dir.append((a, root * (t_i * uh + uh)))
        af, bf = ab_dir[0]
        hf_fin = af * hf_fin + bf
        pf_tot = af * pf_tot
        a_f[rows(i), :] = af
        h_f[rows(i), :] = hf_fin
        ab, bb = ab_dir[1]
        if i == brk_i:
            ab = jnp.where(seg_id == brk_s, 0.0, ab)
        a_b[rows(i), :] = ab
        b_b[rows(i), :] = bb

    hin = jnp.zeros((1, LANES), F32)
    hin_f = []
    for s in range(nseg):
        hin_f.append(hin)
        hin = hf_fin[s:s + 1, :] + pf_tot[s:s + 1, :] * hin
    hin_f = jnp.concatenate(hin_f, axis=0)

    hb_fin, pb_tot, pf = zeros, ones, ones
    for i in range(seg):
        j = seg - 1 - i
        ab = a_b[rows(j), :]
        hb_fin = ab * hb_fin + b_b[rows(j), :]
        pb_tot = ab * pb_tot
        h_b[rows(j), :] = hb_fin
        pf = a_f[rows(i), :] * pf
        h_f[rows(i), :] = h_f[rows(i), :] + pf * hin_f

    order = [(brk_s - k) % nseg for k in range(nseg)]
    hin_b = [None] * nseg
    hin = jnp.zeros((1, LANES), F32)
    for s in order:
        hin_b[s] = hin
        hin = hb_fin[s:s + 1, :] + pb_tot[s:s + 1, :] * hin
    hin_b[brk_s] = hin
    hin_b = jnp.concatenate(hin_b, axis=0)

    pb = ones
    for i in range(seg):
        j = seg - 1 - i
        pb = a_b[rows(j), :] * pb
        y32[rows(j), :] = h_f[rows(j), :] + (h_b[rows(j), :] + pb * hin_b)
    y_ref[...] = y32[...].astype(BF16)


def _lru_call(l, u, conv_w, conv_b, wcat, lam, *, n_ctx):
    b, t, w = u.shape
    n_lat = t - n_ctx
    assert t % SCAN_SEGMENTS == 0 and w == LRU_BLOCKS * LANES and n_ctx > 0
    scr = [pltpu.VMEM((t + 2 * SUBLANES, LANES), F32)] + [pltpu.VMEM((t, LANES), F32) for _ in range(6)]
    return pl.pallas_call(
        functools.partial(_lru_body, n_ctx=n_ctx, n_lat=n_lat),
        grid=(b, LRU_BLOCKS),
        in_specs=[pl.BlockSpec((None, t, LANES), lambda bi, ci: (bi, 0, ci)),
                  pl.BlockSpec((None, CONV_W, LANES), lambda bi, ci: (l, 0, ci)),
                  pl.BlockSpec((None, 1, LANES), lambda bi, ci: (l, 0, ci)),
                  pl.BlockSpec((None, None, 2 * LANES, 4 * LANES), lambda bi, ci: (l, ci, 0, 0)),
                  pl.BlockSpec((None, 2, LANES), lambda bi, ci: (l, 0, ci))],
        out_specs=pl.BlockSpec((None, t, LANES), lambda bi, ci: (bi, 0, ci)),
        out_shape=jax.ShapeDtypeStruct((b, t, w), BF16),
        scratch_shapes=scr,
        compiler_params=_params(("parallel", "parallel")),
        name="rg_lru",
    )(u, conv_w, conv_b, wcat, lam)


def _merge_body(x_ref, att_ref, y_ref, gz_ref, sa_ref, sr_ref, mod_ref, g_ref,
                wab_ref, wrb_ref, wo_ref, wrh_ref, wrl_ref, xo_ref, h2_ref, aff_ref, mix_s):
    tt, d = x_ref.shape
    ch = 2 * LANES
    n_exp = aff_ref.shape[0]
    att = att_ref[...]
    rec_in = y_ref[...] * gz_ref[...]
    for c0 in range(0, d, ch):
        cols = slice(c0, c0 + ch)
        att_d = jnp.dot(att, wab_ref[:, cols], preferred_element_type=F32)
        rec_d = jnp.dot(rec_in, wrb_ref[:, cols], preferred_element_type=F32)
        mix = sa_ref[:, cols].astype(F32) * att_d + sr_ref[:, cols].astype(F32) * rec_d
        mix_s[:, cols] = mix.astype(BF16)
    mix = mix_s[...]
    part = jnp.zeros((tt, LANES), F32)
    for c0 in range(0, d, ch):
        cols = slice(c0, c0 + ch)
        out = jnp.dot(mix, wo_ref[:, cols], preferred_element_type=F32)
        xn = x_ref[:, cols] + mod_ref[2:3, cols] * out
        xo_ref[:, cols] = xn
        sq = xn * xn
        part = part + sq[:, 0:LANES] + sq[:, LANES:ch]
    rs = lax.rsqrt(jnp.sum(part, axis=-1, keepdims=True) * (1.0 / d) + EPS)
    logits = jnp.zeros((tt, LANES), F32)
    for c0 in range(0, d, ch):
        cols = slice(c0, c0 + ch)
        h2 = xo_ref[:, cols] * rs * g_ref[:, cols] * (1.0 + mod_ref[4:5, cols]) + mod_ref[3:4, cols]
        hi = h2.astype(BF16)
        h2_ref[:, cols] = hi
        lo = (h2 - hi.astype(F32)).astype(BF16)
        logits = (logits + jnp.dot(hi, wrh_ref[cols, :], preferred_element_type=F32)
                  + jnp.dot(lo, wrh_ref[cols, :], preferred_element_type=F32)
                  + jnp.dot(hi, wrl_ref[cols, :], preferred_element_type=F32))
    lt = logits.T[0:n_exp, :]
    mx = jnp.max(lt, axis=0, keepdims=True)
    ex = jnp.exp(lt - mx)
    aff_ref[...] = ex / jnp.sum(ex, axis=0, keepdims=True)


def _merge_call(l, xc, att, y, gz, sa, sr, modcat, g, wab, wrb, wo, wr_hi, wr_lo, *, n_ctx, tt, with_ctx):
    b, t, d = xc.shape
    nct = n_ctx // tt
    off = 0 if with_ctx else nct
    t_out = t - off * tt
    row = lambda bi, ti: (bi, ti + off, 0)
    orow = lambda bi, ti: (bi, ti, 0)
    lay = lambda bi, ti: (l, 0, 0)
    return pl.pallas_call(
        _merge_body,
        grid=(b, t // tt - off),
        in_specs=[pl.BlockSpec((None, tt, d), row), pl.BlockSpec((None, tt, d), orow)]
        + [pl.BlockSpec((None, tt, d), row)] * 4 + [
            pl.BlockSpec((None, None, None, N_MOD, d),
                         lambda bi, ti: (l, bi, jnp.where(ti + off >= nct, 1, 0), 0, 0)),
            pl.BlockSpec((None, 1, d), lay),
            pl.BlockSpec((None,) + wab.shape[1:], lay), pl.BlockSpec((None,) + wrb.shape[1:], lay),
            pl.BlockSpec((None,) + wo.shape[1:], lay),
            pl.BlockSpec((None, d, LANES), lay), pl.BlockSpec((None, d, LANES), lay)],
        out_specs=[pl.BlockSpec((None, tt, d), orow), pl.BlockSpec((None, tt, d), orow),
                   pl.BlockSpec((None, N_EXPERTS, tt), lambda bi, ti: (bi, 0, ti))],
        out_shape=[jax.ShapeDtypeStruct((b, t_out, d), F32), jax.ShapeDtypeStruct((b, t_out, d), BF16),
                   jax.ShapeDtypeStruct((b, N_EXPERTS, t_out), F32)],
        scratch_shapes=[pltpu.VMEM((tt, d), BF16)],
        compiler_params=_params(("parallel", "parallel")),
        name="merge_router",
    )(xc, att, y, gz, sa, sr, modcat, g, wab, wrb, wo, wr_hi, wr_lo)


def _prefix_excl(mask, tri):
    rows, length = mask.shape
    run = jnp.zeros((rows, 1), F32)
    parts = []
    for kb in range(length // LANES):
        blk = mask[:, kb * LANES:(kb + 1) * LANES]
        loc = jnp.dot(blk.astype(BF16), tri, preferred_element_type=F32)
        parts.append(loc + run)
        run = run + jnp.sum(blk, axis=1, keepdims=True)
    return jnp.concatenate(parts, axis=1)


def _count_ge(aff, cand):
    return jnp.sum(jnp.where(aff >= cand, 1.0, 0.0), axis=1, keepdims=True)


def _select_topcap(segments, tri):
    n = len(segments)
    affs = [s[0] for s in segments]
    caps = [float(s[1]) for s in segments]
    rows = affs[0].shape[0]
    base = jnp.full((rows, 1), MIN_NORMAL, F32)
    has = [_count_ge(affs[i], base) >= caps[i] for i in range(n)]

    def best(i, cur, cands):
        for c in cands:
            cur = jnp.where(_count_ge(affs[i], c) >= caps[i], c, cur)
        return cur

    lo = [base] * n
    for j in range(6, -1, -2):
        fa = float(2.0 ** (2 ** j))
        fb = float(2.0 ** (2 ** (j - 1))) if j > 0 else None
        for i in range(n):
            mults = [fa] if fb is None else [fb, fa, fa * fb]
            lo[i] = best(i, lo[i], [lo[i] * m for m in mults])
    lo = [jnp.where(has[i], lo[i], 0.0) for i in range(n)]
    cur = list(lo)
    for j in range(1, MANTISSA_STEPS + 1, 2):
        q = float(2.0 ** -(j + 1))
        for i in range(n):
            cur[i] = best(i, cur[i], [cur[i] + lo[i] * (k * q) for k in (1, 2, 3)])
    codes = []
    for i in range(n):
        aff = affs[i]
        nxt = jnp.where(has[i], cur[i] + lo[i] * float(2.0 ** -23), base)
        ge = jnp.where(aff >= cur[i], 1.0, 0.0)
        tie = ge * jnp.where(aff < nxt, 1.0, 0.0)
        excess = jnp.sum(ge, axis=1, keepdims=True) - caps[i]
        after = jnp.sum(tie, axis=1, keepdims=True) - (_prefix_excl(tie, tri) + tie)
        sel = ge - tie * jnp.where(after < excess, 1.0, 0.0)
        pos = _prefix_excl(sel, tri)
        codes.append(jnp.where(sel > 0.5, pos, -1.0))
    return codes


def _route_body(aff_ref, code_ref, affrow_ref, bnd_ref, *, n_ctx, n_lat, tt):
    aff = aff_ref[...]
    e = aff.shape[0]
    r = lax.broadcasted_iota(I32, (LANES, LANES), 0)
    c = lax.broadcasted_iota(I32, (LANES, LANES), 1)
    tri = (r < c).astype(BF16)
    cap_lat = EC_FACTOR * n_lat // N_EXPERTS
    if n_ctx:
        cap_ctx = EC_FACTOR * n_ctx // N_EXPERTS
        code_ctx, code = _select_topcap([(aff[:, 0:n_ctx], cap_ctx), (aff[:, n_ctx:n_ctx + n_lat], cap_lat)], tri)
        code = jnp.concatenate([code_ctx, jnp.where(code >= 0.0, code + cap_ctx, -1.0)], axis=1)
    else:
        (code,) = _select_topcap([(aff, cap_lat)], tri)
    taken = jnp.where(code >= 0.0, 1.0, 0.0)
    run = jnp.zeros((e, 1), F32)
    bnds = [run]
    for t0 in range(0, n_ctx + n_lat, tt):
        run = run + jnp.sum(taken[:, t0:t0 + tt], axis=1, keepdims=True)
        bnds.append(run)
    bnd_ref[...] = jnp.concatenate(bnds, axis=1).astype(I32)
    code = code.astype(I32)
    for ei in range(e):
        code_ref[ei] = code[ei:ei + 1, :]
        affrow_ref[ei] = aff[ei:ei + 1, :]


def _route_call(aff, *, n_ctx, tt):
    b, e, t = aff.shape
    spec4 = pl.BlockSpec((None, e, 1, t), lambda bi: (bi, 0, 0, 0))
    nb = t // tt + 1
    return pl.pallas_call(
        functools.partial(_route_body, n_ctx=n_ctx, n_lat=t - n_ctx, tt=tt),
        grid=(b,),
        in_specs=[pl.BlockSpec((None, e, t), lambda bi: (bi, 0, 0))],
        out_specs=[spec4, spec4, pl.BlockSpec((None, e, nb), lambda bi: (bi, 0, 0))],
        out_shape=[jax.ShapeDtypeStruct((b, e, 1, t), I32), jax.ShapeDtypeStruct((b, e, 1, t), F32),
                   jax.ShapeDtypeStruct((b, e, nb), I32)],
        compiler_params=_params(("parallel",)),
        name="route_select",
    )(aff)


def _moe_body(code_ref, aff_ref, h_ref, wg32_ref, wu32_ref, wd32_ref, ye_ref, wg_s, wu_s, wd_s,
              *, n_ctx, n_ctx_slots):
    p = pl.program_id(0)
    bi = pl.program_id(1)
    n_exp = pl.num_programs(0) - 1
    fill = p % 2
    use = 1 - fill

    @pl.when(p < n_exp)
    def _():
        rd = wg32_ref.shape[0]
        rf = wd32_ref.shape[0]
        r0 = pl.multiple_of(bi * rd, rd)
        f0 = pl.multiple_of(bi * rf, rf)
        wg_s[fill, pl.ds(r0, rd), :] = wg32_ref[...].astype(BF16)
        wu_s[fill, pl.ds(r0, rd), :] = wu32_ref[...].astype(BF16)
        wd_s[fill, pl.ds(f0, rf), :] = wd32_ref[...].astype(BF16)

    @pl.when(p == 0)
    def _():
        ye_ref[...] = jnp.zeros(ye_ref.shape, ye_ref.dtype)

    @pl.when(p > 0)
    def _():
        n_slots = ye_ref.shape[0]
        t = code_ref.shape[1]
        xs, gs = [], []
        for (s0, s1, t0, t1) in ((0, n_ctx_slots, 0, n_ctx), (n_ctx_slots, n_slots, n_ctx, t)):
            if s1 == s0:
                continue
            hit = code_ref[:, t0:t1] == s0 + lax.broadcasted_iota(I32, (s1 - s0, t1 - t0), 0)
            sel = jnp.where(hit, 1.0, 0.0).astype(BF16)
            xs.append(jnp.dot(sel, h_ref[t0:t1, :], preferred_element_type=F32).astype(BF16))
            gs.append(jnp.sum(jnp.where(hit, aff_ref[:, t0:t1], 0.0), axis=1, keepdims=True))
        xe = jnp.concatenate(xs, axis=0)
        gcol = jnp.concatenate(gs, axis=0)
        gate = jnp.dot(xe, wg_s[use], preferred_element_type=F32)
        up = jnp.dot(xe, wu_s[use], preferred_element_type=F32)
        hid = (gate * _sigmoid(gate) * up).astype(BF16)
        ye = jnp.dot(hid, wd_s[use], preferred_element_type=F32) * gcol
        ye_ref[...] = ye.astype(BF16)


def _moe_call(l, code, aff4, h2, wg, wu, wd, *, n_slots, n_ctx):
    b, t, d = h2.shape
    _, e, _, f = wg.shape
    assert d % (b * 16) == 0 and f % (b * 16) == 0
    rd, rf = d // b, f // b
    n_ctx_slots = EC_FACTOR * n_ctx // N_EXPERTS
    ex = lambda pi: jnp.maximum(pi - 1, 0)
    wmap = lambda pi, bi: (l, jnp.minimum(pi, e - 1), jnp.where(pi < e, bi, b - 1), 0)
    return pl.pallas_call(
        functools.partial(_moe_body, n_ctx=n_ctx, n_ctx_slots=n_ctx_slots),
        grid=(e + 1, b),
        in_specs=[pl.BlockSpec((None, None, 1, t), lambda pi, bi: (bi, ex(pi), 0, 0)),
                  pl.BlockSpec((None, None, 1, t), lambda pi, bi: (bi, ex(pi), 0, 0)),
                  pl.BlockSpec((None, t, d), lambda pi, bi: (bi, 0, 0)),
                  pl.BlockSpec((None, None, rd, f), wmap),
                  pl.BlockSpec((None, None, rd, f), wmap),
                  pl.BlockSpec((None, None, rf, d), wmap)],
        out_specs=pl.BlockSpec((None, None, n_slots, d), lambda pi, bi: (bi, jnp.where(pi == 0, e, pi - 1), 0, 0)),
        out_shape=jax.ShapeDtypeStruct((b, e + 1, n_slots, d), BF16),
        scratch_shapes=[pltpu.VMEM((2, d, f), BF16), pltpu.VMEM((2, d, f), BF16), pltpu.VMEM((2, f, d), BF16)],
        compiler_params=_params(("arbitrary", "arbitrary")),
        name="moe_ffn",
    )(code, aff4, h2, wg, wu, wd)


def _combine_body(bnd_ref, x_ref, code_ref, ye_ref, mod_ref, gf_ref, o_ref, *, n_slots, final, win):
    bi = pl.program_id(0)
    ti = pl.program_id(1)
    nb = pl.num_programs(1) + 1
    e = ye_ref.shape[0]
    tt, d = x_ref.shape
    tn = (((0,), (0,)), ((), ()))
    starts = []
    fits = None
    for ei in range(e):
        at = (bi * e + ei) * nb + ti
        lo = bnd_ref[at]
        hi = bnd_ref[at + 1]
        st = jnp.minimum(lax.shift_left(lax.shift_right_logical(lo, 4), 4), n_slots - win)
        starts.append(pl.multiple_of(st, 16))
        ok = hi - st <= win
        fits = ok if fits is None else jnp.logical_and(fits, ok)

    def finish(acc):
        xn = x_ref[...] + mod_ref[5:6, :] * acc
        if final:
            xn = _rmsnorm(xn, gf_ref[...])
        o_ref[...] = xn

    @pl.when(fits)
    def _():
        slot = lax.broadcasted_iota(I32, (win, tt), 0)
        sel = jnp.concatenate(
            [jnp.where(code_ref[ei] - starts[ei] == slot, 1.0, 0.0).astype(BF16) for ei in range(e)], axis=0)
        ye = jnp.concatenate([ye_ref[ei, pl.ds(starts[ei], win), :] for ei in range(e)], axis=0)
        finish(lax.dot_general(sel, ye, tn, preferred_element_type=F32))

    @pl.when(jnp.logical_not(fits))
    def _():
        slot = lax.broadcasted_iota(I32, (n_slots, tt), 0)
        sel = jnp.concatenate(
            [jnp.where(code_ref[ei] == slot, 1.0, 0.0).astype(BF16) for ei in range(e)], axis=0)
        ye = ye_ref[...].reshape(e * n_slots, d)
        finish(lax.dot_general(sel, ye, tn, preferred_element_type=F32))


def _combine_call(l, xc, code, bounds, ye, modcat, gfinal, *, n_ctx, tt, final):
    b, t, d = xc.shape
    e, n_slots = N_EXPERTS, ye.shape[2]
    nct = n_ctx // tt
    win = min(COMBINE_WINDOW, n_slots)
    grid_spec = pltpu.PrefetchScalarGridSpec(
        num_scalar_prefetch=1,
        grid=(b, t // tt),
        in_specs=[pl.BlockSpec((None, tt, d), lambda bi, ti, bnd: (bi, ti, 0)),
                  pl.BlockSpec((None, e, 1, tt), lambda bi, ti, bnd: (bi, 0, 0, ti)),
                  pl.BlockSpec((None, e, n_slots, d), lambda bi, ti, bnd: (bi, 0, 0, 0)),
                  pl.BlockSpec((None, None, None, N_MOD, d),
                               lambda bi, ti, bnd: (l, bi, jnp.where(ti >= nct, 1, 0), 0, 0)),
                  pl.BlockSpec((1, d), lambda bi, ti, bnd: (0, 0))],
        out_specs=pl.BlockSpec((None, tt, d), lambda bi, ti, bnd: (bi, ti, 0)))
    return pl.pallas_call(
        functools.partial(_combine_body, n_slots=n_slots, final=final, win=win),
        grid_spec=grid_spec,
        out_shape=jax.ShapeDtypeStruct((b, t, d), F32),
        compiler_params=_params(("parallel", "arbitrary")),
        name="moe_combine",
    )(bounds.reshape(-1), xc, code, ye, modcat, gfinal)


def _rope_tables(n_ctx, n_lat):
    quarter = HEAD_DIM // 4
    t = jnp.arange(n_lat, dtype=jnp.int32)
    rows = (t // GRID_W).astype(F32)
    cols = (t % GRID_W).astype(F32)
    freqs = ROPE_THETA ** (-jnp.arange(quarter, dtype=F32) / quarter)
    ang_r = rows[:, None] * freqs[None, :]
    ang_c = cols[:, None] * freqs[None, :]
    cos = jnp.concatenate([jnp.cos(ang_r), jnp.cos(ang_r), jnp.cos(ang_c), jnp.cos(ang_c)], axis=1)
    sin = jnp.concatenate([-jnp.sin(ang_r), jnp.sin(ang_r), -jnp.sin(ang_c), jnp.sin(ang_c)], axis=1)
    cos = jnp.concatenate([jnp.ones((n_ctx, HEAD_DIM), F32), cos], axis=0)
    sin = jnp.concatenate([jnp.zeros((n_ctx, HEAD_DIM), F32), sin], axis=0)
    return cos, sin


def kernel(x, c, ctx, c_ctx, ada_w, ada_b, norm_mix_g, w_in, attn_sink, conv_w, conv_b, lru_wa, lru_ba, lru_wx,
           lru_bx, lru_lambda, w_attn_br, w_rec_br, w_out, norm_ffn_g, w_router, w_gate, w_up, w_down,
           final_norm_g):
    b, n_lat, d = x.shape
    n_ctx = ctx.shape[1]
    depth = ada_w.shape[0]
    tt = math.gcd(math.gcd(n_ctx, n_lat), 256)

    pad_rows = (-(b + 1)) % SUBLANES
    cc = jnp.concatenate([c, c_ctx[None, :], jnp.zeros((pad_rows, d), F32)], axis=0)
    mod = _ada_call(cc, ada_w, ada_b)
    mod_lat = mod[:, :b].reshape(depth, b, 1, N_MOD, d)
    mod_ctx = jnp.broadcast_to(mod[:, b].reshape(depth, 1, 1, N_MOD, d), (depth, b, 1, N_MOD, d))
    modcat = jnp.concatenate([mod_ctx, mod_lat], axis=2)

    cos_t, sin_t = _rope_tables(n_ctx, n_lat)
    xc = jnp.concatenate([ctx, x], axis=1)

    w_in_bf = w_in.astype(BF16)
    wab_bf = w_attn_br.astype(BF16)
    wrb_bf = w_rec_br.astype(BF16)
    wo_bf = w_out.astype(BF16)
    wcat = (0.5 * jnp.concatenate([lru_wa[:, 0], lru_wx[:, 0], lru_wa[:, 1], lru_wx[:, 1]], axis=-1)).astype(BF16)
    bcat = 0.5 * jnp.stack([lru_ba[:, 0], lru_bx[:, 0], lru_ba[:, 1], lru_bx[:, 1]], axis=1)
    bcat = bcat.reshape(depth, 4, LRU_BLOCKS, LANES).transpose(0, 2, 1, 3).reshape(depth, LRU_BLOCKS, 1, 4 * LANES)
    b_hi = bcat.astype(BF16)
    b_lo = (bcat - b_hi.astype(F32)).astype(BF16)
    wcat = jnp.concatenate(
        [wcat, b_hi, b_lo, jnp.zeros((depth, LRU_BLOCKS, LANES - 2, 4 * LANES), BF16)], axis=2)
    sink_rows = jnp.broadcast_to(attn_sink[:, :, None], (depth, N_HEADS, LANES))
    wr_pad = jnp.pad(w_router, ((0, 0), (0, 0), (0, LANES - w_router.shape[2])))
    wr_hi = wr_pad.astype(BF16)
    wr_lo = (wr_pad - wr_hi.astype(F32)).astype(BF16)
    g_mix = norm_mix_g[:, None, :]
    g_ffn = norm_ffn_g[:, None, :]
    conv_b3 = conv_b[:, None, :]

    cap_lat = EC_FACTOR * n_lat // N_EXPERTS
    cap_ctx = EC_FACTOR * n_ctx // N_EXPERTS
    for l in range(depth):
        last = l == depth - 1
        with_ctx = not last
        q, k, v, u, gz, sa, sr = _inproj_call(l, xc, modcat, g_mix, w_in_bf, cos_t, sin_t, n_ctx=n_ctx)
        att = _attn_call(l, q, k, v, sink_rows, n_ctx=n_ctx, with_ctx=with_ctx)
        y = _lru_call(l, u, conv_w, conv_b3, wcat, lru_lambda, n_ctx=n_ctx)
        xc, h2, aff = _merge_call(l, xc, att, y, gz, sa, sr, modcat, g_ffn, wab_bf, wrb_bf, wo_bf, wr_hi, wr_lo,
                                  n_ctx=n_ctx, tt=tt, with_ctx=with_ctx)
        ctx_rows = n_ctx if with_ctx else 0
        code, aff_rows, bounds = _route_call(aff, n_ctx=ctx_rows, tt=tt)
        n_slots = cap_lat + (cap_ctx if with_ctx else 0)
        ye = _moe_call(l, code, aff_rows, h2, w_gate, w_up, w_down, n_slots=n_slots, n_ctx=ctx_rows)
        xc = _combine_call(l, xc, code, bounds, ye, modcat, final_norm_g[None, :], n_ctx=ctx_rows, tt=tt, final=last)
    return xc
```

```python
import functools
import math

import jax
import jax.numpy as jnp
from jax import lax
from jax.experimental import pallas as pl
from jax.experimental.pallas import tpu as pltpu

F32 = jnp.float32
BF16 = jnp.bfloat16
I32 = jnp.int32

HEAD_DIM = 128
N_HEADS = 8
N_KV_HEADS = 2
GROUP = N_HEADS // N_KV_HEADS
ATT_BLOCK = 128
GRID_W = 64
ROPE_THETA = 10000.0
LRU_BLOCKS = 8
LRU_C = 8.0
CONV_W = 4
CONV_LEFT = 2
N_EXPERTS = 16
EC_FACTOR = 2
N_MOD = 6
EPS = 1e-6
NEG_BIG = -1e30
LOG2E = 1.4426950408889634
QK_SCALE = HEAD_DIM ** -0.5 * LOG2E

LANES = 128
SUBLANES = 8
SCAN_SEGMENTS = 64
INPROJ_MAX_ROWS = 768
INPROJ_ROW_PARTS = 2
ATT_BLOCKS_PER_STEP = 2
COMBINE_WINDOW = 80
MIN_NORMAL = 2.0 ** -126
MANTISSA_STEPS = 32
VMEM_LIMIT = 56 * 1024 * 1024


def _sigmoid(x):
    return 0.5 * jnp.tanh(0.5 * x) + 0.5


def _gelu_tanh(x):
    c = math.sqrt(2.0 / math.pi)
    return 0.5 * x * (1.0 + jnp.tanh(c * (x + 0.044715 * (x * x * x))))


def _rmsnorm(x, g):
    ms = jnp.mean(x * x, axis=-1, keepdims=True)
    return x * lax.rsqrt(ms + EPS) * g


def _params(sem):
    return pltpu.CompilerParams(dimension_semantics=sem, vmem_limit_bytes=VMEM_LIMIT)


def _ada_body(c_ref, w_ref, b_ref, o_ref):
    c = c_ref[...]
    act = c * _sigmoid(c)
    o_ref[...] = jnp.dot(act, w_ref[...], preferred_element_type=F32,
                         precision=lax.Precision.HIGHEST) + b_ref[...]


def _ada_call(cc, ada_w, ada_b):
    depth, d, n6 = ada_w.shape
    rows = cc.shape[0]
    tn = 1024
    return pl.pallas_call(
        _ada_body,
        grid=(depth, n6 // tn),
        in_specs=[pl.BlockSpec((rows, d), lambda l, j: (0, 0)),
                  pl.BlockSpec((None, d, tn), lambda l, j: (l, 0, j)),
                  pl.BlockSpec((None, 1, tn), lambda l, j: (l, 0, j))],
        out_specs=pl.BlockSpec((None, rows, tn), lambda l, j: (l, 0, j)),
        out_shape=jax.ShapeDtypeStruct((depth, rows, n6), F32),
        compiler_params=_params(("parallel", "parallel")),
        name="ada_mod",
    )(cc, ada_w, ada_b.reshape(depth, 1, n6))


def _rope(xh, cos, ssin, hi):
    sw = jnp.where(hi, pltpu.roll(xh, 32, 1), pltpu.roll(xh, LANES - 32, 1))
    return xh * cos + sw * ssin


def _inproj_body(x_ref, mod_ref, g_ref, w_ref, cos_ref, sin_ref,
                 q_ref, k_ref, v_ref, u_ref, z_ref, ma_ref, mr_ref, *, d, n_ctx):
    tt = x_ref.shape[0]
    qw = N_HEADS * HEAD_DIM
    kw = N_KV_HEADS * HEAD_DIM
    parts = INPROJ_ROW_PARTS if tt % (16 * INPROJ_ROW_PARTS) == 0 else 1
    tp = tt // parts
    for r0 in range(0, tt, tp):
        rows = slice(r0, r0 + tp)
        x = x_ref[rows, :]
        is_ctx = pl.program_id(1) * tt + r0 + lax.broadcasted_iota(I32, (tp, 1), 0) < n_ctx
        shift = jnp.where(is_ctx, mod_ref[0, 0:1, :], mod_ref[1, 0:1, :])
        scale = jnp.where(is_ctx, mod_ref[0, 1:2, :], mod_ref[1, 1:2, :])
        h = _rmsnorm(x, g_ref[...]) * (1.0 + scale) + shift
        hb = h.astype(BF16)
        cos = cos_ref[rows, :]
        ssin = sin_ref[rows, :]
        cos_q = cos * QK_SCALE
        ssin_q = ssin * QK_SCALE
        hi = (lax.broadcasted_iota(I32, cos.shape, 1) & 32) != 0

        def proj(lo, width, hb=hb):
            return jnp.dot(hb, w_ref[:, lo:lo + width], preferred_element_type=F32)

        for h0 in range(0, N_HEADS, 2):
            qq = proj(h0 * HEAD_DIM, 2 * HEAD_DIM)
            for j in range(2):
                q_ref[rows, (h0 + j) * HEAD_DIM:(h0 + j + 1) * HEAD_DIM] = _rope(
                    qq[:, j * HEAD_DIM:(j + 1) * HEAD_DIM], cos_q, ssin_q, hi).astype(BF16)
        for h0 in range(0, N_KV_HEADS, 2):
            kk = proj(qw + h0 * HEAD_DIM, 2 * HEAD_DIM)
            for j in range(2):
                k_ref[rows, (h0 + j) * HEAD_DIM:(h0 + j + 1) * HEAD_DIM] = _rope(
                    kk[:, j * HEAD_DIM:(j + 1) * HEAD_DIM], cos, ssin, hi).astype(BF16)
        off = qw + kw
        v_ref[rows, :] = proj(off, kw).astype(BF16)
        off += kw
        u_ref[rows, :] = proj(off, d)
        off += d
        for c0 in range(0, d, 2 * LANES):
            z_ref[rows, c0:c0 + 2 * LANES] = _gelu_tanh(proj(off + c0, 2 * LANES)).astype(BF16)
        off += d
        for c0 in range(0, d, 2 * LANES):
            ma_ref[rows, c0:c0 + 2 * LANES] = _sigmoid(proj(off + c0, 2 * LANES)).astype(BF16)
        off += d
        for c0 in range(0, d, 2 * LANES):
            mr_ref[rows, c0:c0 + 2 * LANES] = _sigmoid(proj(off + c0, 2 * LANES)).astype(BF16)


def _inproj_call(l, xc, modcat, g, w_in_bf, cos_t, sin_t, *, n_ctx):
    b, t, d = xc.shape
    in_w = w_in_bf.shape[2]
    tt = max(c for c in range(16, INPROJ_MAX_ROWS + 1, 16) if t % c == 0)
    qw = N_HEADS * HEAD_DIM
    kw = N_KV_HEADS * HEAD_DIM
    row = lambda bi, ti: (bi, ti, 0)
    outs = [jax.ShapeDtypeStruct((b, t, qw), BF16), jax.ShapeDtypeStruct((b, t, kw), BF16),
            jax.ShapeDtypeStruct((b, t, kw), BF16), jax.ShapeDtypeStruct((b, t, d), F32),
            jax.ShapeDtypeStruct((b, t, d), BF16), jax.ShapeDtypeStruct((b, t, d), BF16),
            jax.ShapeDtypeStruct((b, t, d), BF16)]
    return pl.pallas_call(
        functools.partial(_inproj_body, d=d, n_ctx=n_ctx),
        grid=(b, t // tt),
        in_specs=[pl.BlockSpec((None, tt, d), row),
                  pl.BlockSpec((None, None, 2, N_MOD, d), lambda bi, ti: (l, bi, 0, 0, 0)),
                  pl.BlockSpec((None, 1, d), lambda bi, ti: (l, 0, 0)),
                  pl.BlockSpec((None, d, in_w), lambda bi, ti: (l, 0, 0)),
                  pl.BlockSpec((tt, HEAD_DIM), lambda bi, ti: (ti, 0)),
                  pl.BlockSpec((tt, HEAD_DIM), lambda bi, ti: (ti, 0))],
        out_specs=[pl.BlockSpec((None, tt, s.shape[2]), row) for s in outs],
        out_shape=outs,
        compiler_params=_params(("parallel", "parallel")),
        name="in_proj",
    )(xc, modcat, g, w_in_bf, cos_t, sin_t)


def _attn_body(q_ref, k_ref, v_ref, sink_ref, o_ref, *, n_ctx, t_all, blk_off, qb):
    nc = n_ctx // ATT_BLOCK
    band = 3 * ATT_BLOCK
    nt = (((1,), (1,)), ((), ()))
    tn = (((0,), (0,)), ((), ()))
    for j in range(qb):
        n = pl.program_id(1) * qb + j + blk_off
        qrows = slice(j * ATT_BLOCK, (j + 1) * ATT_BLOCK)
        start = jnp.clip(n_ctx + (n - nc - 1) * ATT_BLOCK, 0, t_all - band)
        start = pl.multiple_of(start, ATT_BLOCK)

        kpos = start - n_ctx + lax.broadcasted_iota(I32, (band, ATT_BLOCK), 0)
        qpos = (n - nc) * ATT_BLOCK + lax.broadcasted_iota(I32, (band, ATT_BLOCK), 1)
        valid = (kpos >= 0) & (jnp.abs(qpos - kpos) <= ATT_BLOCK) & (n >= nc)
        bias = jnp.where(valid, 0.0, NEG_BIG)
        bias = jnp.concatenate([bias] * GROUP, axis=1)

        for kh in range(N_KV_HEADS):
            lanes = slice(kh * HEAD_DIM, (kh + 1) * HEAD_DIM)
            qs = jnp.concatenate(
                [q_ref[qrows, (kh * GROUP + g) * HEAD_DIM:(kh * GROUP + g + 1) * HEAD_DIM] for g in range(GROUP)],
                axis=0)
            kb = k_ref[pl.ds(start, band), lanes]
            vb = v_ref[pl.ds(start, band), lanes]
            kc = k_ref[0:n_ctx, lanes]
            vc = v_ref[0:n_ctx, lanes]
            s_band = lax.dot_general(kb, qs, nt, preferred_element_type=F32) + bias
            s_ctx = lax.dot_general(kc, qs, nt, preferred_element_type=F32)
            sink = jnp.concatenate(
                [sink_ref[kh * GROUP + g:kh * GROUP + g + 1, :] for g in range(GROUP)], axis=1) * LOG2E
            m = jnp.maximum(sink, jnp.maximum(jnp.max(s_band, axis=0, keepdims=True),
                                              jnp.max(s_ctx, axis=0, keepdims=True)))
            p_band = jnp.exp2(s_band - m)
            p_ctx = jnp.exp2(s_ctx - m)
            denom = (jnp.exp2(sink - m) + jnp.sum(p_band, axis=0, keepdims=True)
                     + jnp.sum(p_ctx, axis=0, keepdims=True))
            o_t = (lax.dot_general(vb, p_band.astype(BF16), tn, preferred_element_type=F32)
                   + lax.dot_general(vc, p_ctx.astype(BF16), tn, preferred_element_type=F32)) * (1.0 / denom)
            for g in range(GROUP):
                hh = kh * GROUP + g
                o_ref[qrows, hh * HEAD_DIM:(hh + 1) * HEAD_DIM] = (
                    o_t[:, g * ATT_BLOCK:(g + 1) * ATT_BLOCK].T.astype(BF16))


def _attn_call(l, q, k, v, sink_rows, *, n_ctx, with_ctx):
    b, t, qw = q.shape
    kw = k.shape[2]
    blk_off = 0 if with_ctx else n_ctx // ATT_BLOCK
    nblk = t // ATT_BLOCK - blk_off
    qb = ATT_BLOCKS_PER_STEP if nblk % ATT_BLOCKS_PER_STEP == 0 and blk_off % ATT_BLOCKS_PER_STEP == 0 else 1
    rows = qb * ATT_BLOCK
    return pl.pallas_call(
        functools.partial(_attn_body, n_ctx=n_ctx, t_all=t, blk_off=blk_off, qb=qb),
        grid=(b, nblk // qb),
        in_specs=[pl.BlockSpec((None, rows, qw), lambda bi, ni: (bi, ni + blk_off // qb, 0)),
                  pl.BlockSpec((None, t, kw), lambda bi, ni: (bi, 0, 0)),
                  pl.BlockSpec((None, t, kw), lambda bi, ni: (bi, 0, 0)),
                  pl.BlockSpec((None, N_HEADS, LANES), lambda bi, ni: (l, 0, 0))],
        out_specs=pl.BlockSpec((None, rows, qw), lambda bi, ni: (bi, ni, 0)),
        out_shape=jax.ShapeDtypeStruct((b, nblk * ATT_BLOCK, qw), BF16),
        compiler_params=_params(("parallel", "arbitrary")),
        name="win_attn",
    )(q, k, v, sink_rows)


def _lru_body(u_ref, cw_ref, cb_ref, w_ref, lam_ref, y_ref,
              p_f, a_b, b_b, h_f, h_b, y32, *, n_ctx, n_lat):
    t_all = n_ctx + n_lat
    nseg = SCAN_SEGMENTS
    seg = t_all // nseg

    lam = lam_ref[...]
    nl = -lam
    sp = jnp.maximum(nl, 0.0) + jnp.log(1.0 + jnp.exp(-jnp.abs(nl)))
    c2 = sp * (-0.5 * LRU_C * LOG2E)
    cw = cw_ref[...]
    cb = cb_ref[...]
    wcat = w_ref[...]
    lane = lax.broadcasted_iota(I32, (SCAN_SEGMENTS, LANES), 1)
    bias_cols = jnp.where(lane < 2, 1.0, 0.0).astype(BF16)

    def rows(i):
        return pl.ds(i, nseg, stride=seg)

    seg_id = lax.broadcasted_iota(I32, (nseg, LANES), 0)
    ones = jnp.ones((nseg, LANES), F32)
    zeros = jnp.zeros((nseg, LANES), F32)

    def same_part(r, q):
        return (r < n_ctx) == (q < n_ctx)

    def tap(i, off):
        j = i + off
        if 0 <= j < seg:
            v = u_ref[rows(j), :]
        elif j < 0:
            v = pltpu.roll(u_ref[rows(j + seg), :], 1, 0)
        else:
            v = pltpu.roll(u_ref[rows(j - seg), :], nseg - 1, 0)
        for s in range(nseg):
            r = i + seg * s
            if not (0 <= r + off < t_all and same_part(r, r + off)):
                v = jnp.where(seg_id == s, 0.0, v)
        return v

    brk_s, brk_i = (n_ctx - 1) // seg, (n_ctx - 1) % seg

    hf_fin, pf_tot = zeros, ones
    for i in range(seg):
        uc = cb
        for kk in range(CONV_W):
            uc = uc + cw[kk:kk + 1, :] * tap(i, kk - CONV_LEFT)
        lhs = jnp.concatenate([uc.astype(BF16), bias_cols], axis=1)
        th = jnp.tanh(jnp.dot(lhs, wcat, preferred_element_type=F32))
        uh = 0.5 * uc
        ab_dir = []
        for dr in range(2):
            t_r = th[:, (2 * dr) * LANES:(2 * dr + 1) * LANES]
            t_i = th[:, (2 * dr + 1) * LANES:(2 * dr + 2) * LANES]
            cc = c2[dr:dr + 1, :]
            a = jnp.exp2(t_r * cc + cc)
            om = 1.0 - a * a
            root = jnp.where(om > 0.0, om * lax.rsqrt(om), 0.0)
            ab_dir.append((a, root * (t_i * uh + uh)))
        af, bf = ab_dir[0]
        hf_fin = af * hf_fin + bf
        pf_tot = af * pf_tot
        p_f[rows(i), :] = pf_tot
        h_f[rows(i), :] = hf_fin
        ab, bb = ab_dir[1]
        if i == brk_i:
            ab = jnp.where(seg_id == brk_s, 0.0, ab)
        a_b[rows(i), :] = ab
        b_b[rows(i), :] = bb

    hin = jnp.zeros((1, LANES), F32)
    hin_f = []
    for s in range(nseg):
        hin_f.append(hin)
        hin = hf_fin[s:s + 1, :] + pf_tot[s:s + 1, :] * hin
    hin_f = jnp.concatenate(hin_f, axis=0)

    hb_fin, pb_tot = zeros, ones
    for i in range(seg):
        j = seg - 1 - i
        ab = a_b[rows(j), :]
        hb_fin = ab * hb_fin + b_b[rows(j), :]
        pb_tot = ab * pb_tot
        a_b[rows(j), :] = pb_tot
        h_b[rows(j), :] = hb_fin
        h_f[rows(i), :] = h_f[rows(i), :] + p_f[rows(i), :] * hin_f

    order = [(brk_s - k) % nseg for k in range(nseg)]
    hin_b = [None] * nseg
    hin = jnp.zeros((1, LANES), F32)
    for s in order:
        hin_b[s] = hin
        hin = hb_fin[s:s + 1, :] + pb_tot[s:s + 1, :] * hin
    hin_b[brk_s] = hin
    hin_b = jnp.concatenate(hin_b, axis=0)

    for j in range(seg):
        y32[rows(j), :] = h_f[rows(j), :] + (h_b[rows(j), :] + a_b[rows(j), :] * hin_b)
    y_ref[...] = y32[...].astype(BF16)


def _lru_call(l, u, conv_w, conv_b, wcat, lam, *, n_ctx):
    b, t, w = u.shape
    n_lat = t - n_ctx
    assert t % SCAN_SEGMENTS == 0 and w == LRU_BLOCKS * LANES and n_ctx > 0
    scr = [pltpu.VMEM((t, LANES), F32) for _ in range(6)]
    return pl.pallas_call(
        functools.partial(_lru_body, n_ctx=n_ctx, n_lat=n_lat),
        grid=(b, LRU_BLOCKS),
        in_specs=[pl.BlockSpec((None, t, LANES), lambda bi, ci: (bi, 0, ci)),
                  pl.BlockSpec((None, CONV_W, LANES), lambda bi, ci: (l, 0, ci)),
                  pl.BlockSpec((None, 1, LANES), lambda bi, ci: (l, 0, ci)),
                  pl.BlockSpec((None, None, 2 * LANES, 4 * LANES), lambda bi, ci: (l, ci, 0, 0)),
                  pl.BlockSpec((None, 2, LANES), lambda bi, ci: (l, 0, ci))],
        out_specs=pl.BlockSpec((None, t, LANES), lambda bi, ci: (bi, 0, ci)),
        out_shape=jax.ShapeDtypeStruct((b, t, w), BF16),
        scratch_shapes=scr,
        compiler_params=_params(("parallel", "parallel")),
        name="rg_lru",
    )(u, conv_w, conv_b, wcat, lam)


def _merge_body(x_ref, att_ref, y_ref, gz_ref, sa_ref, sr_ref, mod_ref, g_ref,
                wab_ref, wrb_ref, wo_ref, wrh_ref, wrl_ref, xo_ref, h2_ref, aff_ref, mix_s):
    tt, d = x_ref.shape
    ch = 2 * LANES
    n_exp = aff_ref.shape[0]
    att = att_ref[...]
    rec_in = y_ref[...] * gz_ref[...]
    for c0 in range(0, d, ch):
        cols = slice(c0, c0 + ch)
        att_d = jnp.dot(att, wab_ref[:, cols], preferred_element_type=F32)
        rec_d = jnp.dot(rec_in, wrb_ref[:, cols], preferred_element_type=F32)
        mix = sa_ref[:, cols].astype(F32) * att_d + sr_ref[:, cols].astype(F32) * rec_d
        mix_s[:, cols] = mix.astype(BF16)
    mix = mix_s[...]
    part = jnp.zeros((tt, LANES), F32)
    for c0 in range(0, d, ch):
        cols = slice(c0, c0 + ch)
        out = jnp.dot(mix, wo_ref[:, cols], preferred_element_type=F32)
        xn = x_ref[:, cols] + mod_ref[2:3, cols] * out
        xo_ref[:, cols] = xn
        sq = xn * xn
        part = part + sq[:, 0:LANES] + sq[:, LANES:ch]
    rs = lax.rsqrt(jnp.sum(part, axis=-1, keepdims=True) * (1.0 / d) + EPS)
    logits = jnp.zeros((tt, LANES), F32)
    for c0 in range(0, d, ch):
        cols = slice(c0, c0 + ch)
        h2 = xo_ref[:, cols] * rs * g_ref[:, cols] * (1.0 + mod_ref[4:5, cols]) + mod_ref[3:4, cols]
        hi = h2.astype(BF16)
        h2_ref[:, cols] = hi
        lo = (h2 - hi.astype(F32)).astype(BF16)
        logits = (logits + jnp.dot(hi, wrh_ref[cols, :], preferred_element_type=F32)
                  + jnp.dot(lo, wrh_ref[cols, :], preferred_element_type=F32)
                  + jnp.dot(hi, wrl_ref[cols, :], preferred_element_type=F32))
    lt = logits.T[0:n_exp, :]
    mx = jnp.max(lt, axis=0, keepdims=True)
    ex = jnp.exp(lt - mx)
    aff_ref[...] = ex / jnp.sum(ex, axis=0, keepdims=True)


def _merge_call(l, xc, att, y, gz, sa, sr, modcat, g, wab, wrb, wo, wr_hi, wr_lo, *, n_ctx, tt, with_ctx):
    b, t, d = xc.shape
    nct = n_ctx // tt
    off = 0 if with_ctx else nct
    t_out = t - off * tt
    row = lambda bi, ti: (bi, ti + off, 0)
    orow = lambda bi, ti: (bi, ti, 0)
    lay = lambda bi, ti: (l, 0, 0)
    return pl.pallas_call(
        _merge_body,
        grid=(b, t // tt - off),
        in_specs=[pl.BlockSpec((None, tt, d), row), pl.BlockSpec((None, tt, d), orow)]
        + [pl.BlockSpec((None, tt, d), row)] * 4 + [
            pl.BlockSpec((None, None, None, N_MOD, d),
                         lambda bi, ti: (l, bi, jnp.where(ti + off >= nct, 1, 0), 0, 0)),
            pl.BlockSpec((None, 1, d), lay),
            pl.BlockSpec((None,) + wab.shape[1:], lay), pl.BlockSpec((None,) + wrb.shape[1:], lay),
            pl.BlockSpec((None,) + wo.shape[1:], lay),
            pl.BlockSpec((None, d, LANES), lay), pl.BlockSpec((None, d, LANES), lay)],
        out_specs=[pl.BlockSpec((None, tt, d), orow), pl.BlockSpec((None, tt, d), orow),
                   pl.BlockSpec((None, N_EXPERTS, tt), lambda bi, ti: (bi, 0, ti))],
        out_shape=[jax.ShapeDtypeStruct((b, t_out, d), F32), jax.ShapeDtypeStruct((b, t_out, d), BF16),
                   jax.ShapeDtypeStruct((b, N_EXPERTS, t_out), F32)],
        scratch_shapes=[pltpu.VMEM((tt, d), BF16)],
        compiler_params=_params(("parallel", "parallel")),
        name="merge_router",
    )(xc, att, y, gz, sa, sr, modcat, g, wab, wrb, wo, wr_hi, wr_lo)


def _prefix_excl(mask, tri):
    rows, length = mask.shape
    run = jnp.zeros((rows, 1), F32)
    parts = []
    for kb in range(length // LANES):
        blk = mask[:, kb * LANES:(kb + 1) * LANES]
        loc = jnp.dot(blk.astype(BF16), tri, preferred_element_type=F32)
        parts.append(loc + run)
        run = run + jnp.sum(blk, axis=1, keepdims=True)
    return jnp.concatenate(parts, axis=1)


def _count_ge(aff, cand):
    return jnp.sum(jnp.where(aff >= cand, 1.0, 0.0), axis=1, keepdims=True)


def _select_topcap(segments, tri):
    n = len(segments)
    affs = [s[0] for s in segments]
    caps = [float(s[1]) for s in segments]
    rows = affs[0].shape[0]
    base = jnp.full((rows, 1), MIN_NORMAL, F32)
    has = [_count_ge(affs[i], base) >= caps[i] for i in range(n)]

    def best(i, cur, cands):
        for c in cands:
            cur = jnp.where(_count_ge(affs[i], c) >= caps[i], c, cur)
        return cur

    lo = [base] * n
    for j in range(6, -1, -2):
        fa = float(2.0 ** (2 ** j))
        fb = float(2.0 ** (2 ** (j - 1))) if j > 0 else None
        for i in range(n):
            mults = [fa] if fb is None else [fb, fa, fa * fb]
            lo[i] = best(i, lo[i], [lo[i] * m for m in mults])
    lo = [jnp.where(has[i], lo[i], 0.0) for i in range(n)]
    cur = list(lo)
    for j in range(1, MANTISSA_STEPS + 1, 2):
        q = float(2.0 ** -(j + 1))
        for i in range(n):
            cur[i] = best(i, cur[i], [cur[i] + lo[i] * (k * q) for k in (1, 2, 3)])
    codes = []
    for i in range(n):
        aff = affs[i]
        nxt = jnp.where(has[i], cur[i] + lo[i] * float(2.0 ** -23), base)
        ge = jnp.where(aff >= cur[i], 1.0, 0.0)
        tie = ge * jnp.where(aff < nxt, 1.0, 0.0)
        excess = jnp.sum(ge, axis=1, keepdims=True) - caps[i]
        after = jnp.sum(tie, axis=1, keepdims=True) - (_prefix_excl(tie, tri) + tie)
        sel = ge - tie * jnp.where(after < excess, 1.0, 0.0)
        pos = _prefix_excl(sel, tri)
        codes.append(jnp.where(sel > 0.5, pos, -1.0))
    return codes


def _route_body(aff_ref, code_ref, affrow_ref, bnd_ref, *, n_ctx, n_lat, tt):
    aff = aff_ref[...]
    e = aff.shape[0]
    r = lax.broadcasted_iota(I32, (LANES, LANES), 0)
    c = lax.broadcasted_iota(I32, (LANES, LANES), 1)
    tri = (r < c).astype(BF16)
    cap_lat = EC_FACTOR * n_lat // N_EXPERTS
    if n_ctx:
        cap_ctx = EC_FACTOR * n_ctx // N_EXPERTS
        code_ctx, code = _select_topcap([(aff[:, 0:n_ctx], cap_ctx), (aff[:, n_ctx:n_ctx + n_lat], cap_lat)], tri)
        code = jnp.concatenate([code_ctx, jnp.where(code >= 0.0, code + cap_ctx, -1.0)], axis=1)
    else:
        (code,) = _select_topcap([(aff, cap_lat)], tri)
    taken = jnp.where(code >= 0.0, 1.0, 0.0)
    run = jnp.zeros((e, 1), F32)
    bnds = [run]
    for t0 in range(0, n_ctx + n_lat, tt):
        run = run + jnp.sum(taken[:, t0:t0 + tt], axis=1, keepdims=True)
        bnds.append(run)
    bnd_ref[...] = jnp.concatenate(bnds, axis=1).astype(I32)
    code = code.astype(I32)
    for ei in range(e):
        code_ref[ei] = code[ei:ei + 1, :]
        affrow_ref[ei] = aff[ei:ei + 1, :]


def _route_call(aff, *, n_ctx, tt):
    b, e, t = aff.shape
    spec4 = pl.BlockSpec((None, e, 1, t), lambda bi: (bi, 0, 0, 0))
    nb = t // tt + 1
    return pl.pallas_call(
        functools.partial(_route_body, n_ctx=n_ctx, n_lat=t - n_ctx, tt=tt),
        grid=(b,),
        in_specs=[pl.BlockSpec((None, e, t), lambda bi: (bi, 0, 0))],
        out_specs=[spec4, spec4, pl.BlockSpec((None, e, nb), lambda bi: (bi, 0, 0))],
        out_shape=[jax.ShapeDtypeStruct((b, e, 1, t), I32), jax.ShapeDtypeStruct((b, e, 1, t), F32),
                   jax.ShapeDtypeStruct((b, e, nb), I32)],
        compiler_params=_params(("parallel",)),
        name="route_select",
    )(aff)


def _moe_body(code_ref, aff_ref, h_ref, wg32_ref, wu32_ref, wd32_ref, ye_ref, wg_s, wu_s, wd_s,
              *, n_ctx, n_ctx_slots):
    p = pl.program_id(0)
    bi = pl.program_id(1)
    n_exp = pl.num_programs(0) - 1
    fill = p % 2
    use = 1 - fill

    @pl.when(p < n_exp)
    def _():
        rd = wg32_ref.shape[0]
        rf = wd32_ref.shape[0]
        r0 = pl.multiple_of(bi * rd, rd)
        f0 = pl.multiple_of(bi * rf, rf)
        wg_s[fill, pl.ds(r0, rd), :] = wg32_ref[...].astype(BF16)
        wu_s[fill, pl.ds(r0, rd), :] = wu32_ref[...].astype(BF16)
        wd_s[fill, pl.ds(f0, rf), :] = wd32_ref[...].astype(BF16)

    @pl.when(p == 0)
    def _():
        ye_ref[...] = jnp.zeros(ye_ref.shape, ye_ref.dtype)

    @pl.when(p > 0)
    def _():
        n_slots = ye_ref.shape[0]
        t = code_ref.shape[1]
        xs, gs = [], []
        for (s0, s1, t0, t1) in ((0, n_ctx_slots, 0, n_ctx), (n_ctx_slots, n_slots, n_ctx, t)):
            if s1 == s0:
                continue
            hit = code_ref[:, t0:t1] == s0 + lax.broadcasted_iota(I32, (s1 - s0, t1 - t0), 0)
            sel = jnp.where(hit, 1.0, 0.0).astype(BF16)
            xs.append(jnp.dot(sel, h_ref[t0:t1, :], preferred_element_type=F32).astype(BF16))
            gs.append(jnp.sum(jnp.where(hit, aff_ref[:, t0:t1], 0.0), axis=1, keepdims=True))
        xe = jnp.concatenate(xs, axis=0)
        gcol = jnp.concatenate(gs, axis=0)
        gate = jnp.dot(xe, wg_s[use], preferred_element_type=F32)
        up = jnp.dot(xe, wu_s[use], preferred_element_type=F32)
        hid = (gate * _sigmoid(gate) * up).astype(BF16)
        ye = jnp.dot(hid, wd_s[use], preferred_element_type=F32) * gcol
        ye_ref[...] = ye.astype(BF16)


def _moe_call(l, code, aff4, h2, wg, wu, wd, *, n_slots, n_ctx):
    b, t, d = h2.shape
    _, e, _, f = wg.shape
    assert d % (b * 16) == 0 and f % (b * 16) == 0
    rd, rf = d // b, f // b
    n_ctx_slots = EC_FACTOR * n_ctx // N_EXPERTS
    ex = lambda pi: jnp.maximum(pi - 1, 0)
    wmap = lambda pi, bi: (l, jnp.minimum(pi, e - 1), jnp.where(pi < e, bi, b - 1), 0)
    return pl.pallas_call(
        functools.partial(_moe_body, n_ctx=n_ctx, n_ctx_slots=n_ctx_slots),
        grid=(e + 1, b),
        in_specs=[pl.BlockSpec((None, None, 1, t), lambda pi, bi: (bi, ex(pi), 0, 0)),
                  pl.BlockSpec((None, None, 1, t), lambda pi, bi: (bi, ex(pi), 0, 0)),
                  pl.BlockSpec((None, t, d), lambda pi, bi: (bi, 0, 0)),
                  pl.BlockSpec((None, None, rd, f), wmap),
                  pl.BlockSpec((None, None, rd, f), wmap),
                  pl.BlockSpec((None, None, rf, d), wmap)],
        out_specs=pl.BlockSpec((None, None, n_slots, d), lambda pi, bi: (bi, jnp.where(pi == 0, e, pi - 1), 0, 0)),
        out_shape=jax.ShapeDtypeStruct((b, e + 1, n_slots, d), BF16),
        scratch_shapes=[pltpu.VMEM((2, d, f), BF16), pltpu.VMEM((2, d, f), BF16), pltpu.VMEM((2, f, d), BF16)],
        compiler_params=_params(("arbitrary", "arbitrary")),
        name="moe_ffn",
    )(code, aff4, h2, wg, wu, wd)


def _combine_body(bnd_ref, x_ref, code_ref, ye_ref, mod_ref, gf_ref, o_ref, *, n_slots, final, win):
    bi = pl.program_id(0)
    ti = pl.program_id(1)
    nb = pl.num_programs(1) + 1
    e = ye_ref.shape[0]
    tt, d = x_ref.shape
    tn = (((0,), (0,)), ((), ()))
    starts = []
    fits = None
    for ei in range(e):
        at = (bi * e + ei) * nb + ti
        lo = bnd_ref[at]
        hi = bnd_ref[at + 1]
        st = jnp.minimum(lax.shift_left(lax.shift_right_logical(lo, 4), 4), n_slots - win)
        starts.append(pl.multiple_of(st, 16))
        ok = hi - st <= win
        fits = ok if fits is None else jnp.logical_and(fits, ok)

    def finish(acc):
        xn = x_ref[...] + mod_ref[5:6, :] * acc
        if final:
            xn = _rmsnorm(xn, gf_ref[...])
        o_ref[...] = xn

    @pl.when(fits)
    def _():
        slot = lax.broadcasted_iota(I32, (win, tt), 0)
        sel = jnp.concatenate(
            [jnp.where(code_ref[ei] - starts[ei] == slot, 1.0, 0.0).astype(BF16) for ei in range(e)], axis=0)
        ye = jnp.concatenate([ye_ref[ei, pl.ds(starts[ei], win), :] for ei in range(e)], axis=0)
        finish(lax.dot_general(sel, ye, tn, preferred_element_type=F32))

    @pl.when(jnp.logical_not(fits))
    def _():
        slot = lax.broadcasted_iota(I32, (n_slots, tt), 0)
        sel = jnp.concatenate(
            [jnp.where(code_ref[ei] == slot, 1.0, 0.0).astype(BF16) for ei in range(e)], axis=0)
        ye = ye_ref[...].reshape(e * n_slots, d)
        finish(lax.dot_general(sel, ye, tn, preferred_element_type=F32))


def _combine_call(l, xc, code, bounds, ye, modcat, gfinal, *, n_ctx, tt, final):
    b, t, d = xc.shape
    e, n_slots = N_EXPERTS, ye.shape[2]
    nct = n_ctx // tt
    win = min(COMBINE_WINDOW, n_slots)
    grid_spec = pltpu.PrefetchScalarGridSpec(
        num_scalar_prefetch=1,
        grid=(b, t // tt),
        in_specs=[pl.BlockSpec((None, tt, d), lambda bi, ti, bnd: (bi, ti, 0)),
                  pl.BlockSpec((None, e, 1, tt), lambda bi, ti, bnd: (bi, 0, 0, ti)),
                  pl.BlockSpec((None, e, n_slots, d), lambda bi, ti, bnd: (bi, 0, 0, 0)),
                  pl.BlockSpec((None, None, None, N_MOD, d),
                               lambda bi, ti, bnd: (l, bi, jnp.where(ti >= nct, 1, 0), 0, 0)),
                  pl.BlockSpec((1, d), lambda bi, ti, bnd: (0, 0))],
        out_specs=pl.BlockSpec((None, tt, d), lambda bi, ti, bnd: (bi, ti, 0)))
    return pl.pallas_call(
        functools.partial(_combine_body, n_slots=n_slots, final=final, win=win),
        grid_spec=grid_spec,
        out_shape=jax.ShapeDtypeStruct((b, t, d), F32),
        compiler_params=_params(("parallel", "arbitrary")),
        name="moe_combine",
    )(bounds.reshape(-1), xc, code, ye, modcat, gfinal)


def _rope_tables(n_ctx, n_lat):
    quarter = HEAD_DIM // 4
    t = jnp.arange(n_lat, dtype=jnp.int32)
    rows = (t // GRID_W).astype(F32)
    cols = (t % GRID_W).astype(F32)
    freqs = ROPE_THETA ** (-jnp.arange(quarter, dtype=F32) / quarter)
    ang_r = rows[:, None] * freqs[None, :]
    ang_c = cols[:, None] * freqs[None, :]
    cos = jnp.concatenate([jnp.cos(ang_r), jnp.cos(ang_r), jnp.cos(ang_c), jnp.cos(ang_c)], axis=1)
    sin = jnp.concatenate([-jnp.sin(ang_r), jnp.sin(ang_r), -jnp.sin(ang_c), jnp.sin(ang_c)], axis=1)
    cos = jnp.concatenate([jnp.ones((n_ctx, HEAD_DIM), F32), cos], axis=0)
    sin = jnp.concatenate([jnp.zeros((n_ctx, HEAD_DIM), F32), sin], axis=0)
    return cos, sin


def kernel(x, c, ctx, c_ctx, ada_w, ada_b, norm_mix_g, w_in, attn_sink, conv_w, conv_b, lru_wa, lru_ba, lru_wx,
           lru_bx, lru_lambda, w_attn_br, w_rec_br, w_out, norm_ffn_g, w_router, w_gate, w_up, w_down,
           final_norm_g):
    b, n_lat, d = x.shape
    n_ctx = ctx.shape[1]
    depth = ada_w.shape[0]
    tt = math.gcd(math.gcd(n_ctx, n_lat), 256)

    pad_rows = (-(b + 1)) % SUBLANES
    cc = jnp.concatenate([c, c_ctx[None, :], jnp.zeros((pad_rows, d), F32)], axis=0)
    mod = _ada_call(cc, ada_w, ada_b)
    mod_lat = mod[:, :b].reshape(depth, b, 1, N_MOD, d)
    mod_ctx = jnp.broadcast_to(mod[:, b].reshape(depth, 1, 1, N_MOD, d), (depth, b, 1, N_MOD, d))
    modcat = jnp.concatenate([mod_ctx, mod_lat], axis=2)

    cos_t, sin_t = _rope_tables(n_ctx, n_lat)
    xc = jnp.concatenate([ctx, x], axis=1)

    w_in_bf = w_in.astype(BF16)
    wab_bf = w_attn_br.astype(BF16)
    wrb_bf = w_rec_br.astype(BF16)
    wo_bf = w_out.astype(BF16)
    wcat = (0.5 * jnp.concatenate([lru_wa[:, 0], lru_wx[:, 0], lru_wa[:, 1], lru_wx[:, 1]], axis=-1)).astype(BF16)
    bcat = 0.5 * jnp.stack([lru_ba[:, 0], lru_bx[:, 0], lru_ba[:, 1], lru_bx[:, 1]], axis=1)
    bcat = bcat.reshape(depth, 4, LRU_BLOCKS, LANES).transpose(0, 2, 1, 3).reshape(depth, LRU_BLOCKS, 1, 4 * LANES)
    b_hi = bcat.astype(BF16)
    b_lo = (bcat - b_hi.astype(F32)).astype(BF16)
    wcat = jnp.concatenate(
        [wcat, b_hi, b_lo, jnp.zeros((depth, LRU_BLOCKS, LANES - 2, 4 * LANES), BF16)], axis=2)
    sink_rows = jnp.broadcast_to(attn_sink[:, :, None], (depth, N_HEADS, LANES))
    wr_pad = jnp.pad(w_router, ((0, 0), (0, 0), (0, LANES - w_router.shape[2])))
    wr_hi = wr_pad.astype(BF16)
    wr_lo = (wr_pad - wr_hi.astype(F32)).astype(BF16)
    g_mix = norm_mix_g[:, None, :]
    g_ffn = norm_ffn_g[:, None, :]
    conv_b3 = conv_b[:, None, :]

    cap_lat = EC_FACTOR * n_lat // N_EXPERTS
    cap_ctx = EC_FACTOR * n_ctx // N_EXPERTS
    for l in range(depth):
        last = l == depth - 1
        with_ctx = not last
        q, k, v, u, gz, sa, sr = _inproj_call(l, xc, modcat, g_mix, w_in_bf, cos_t, sin_t, n_ctx=n_ctx)
        att = _attn_call(l, q, k, v, sink_rows, n_ctx=n_ctx, with_ctx=with_ctx)
        y = _lru_call(l, u, conv_w, conv_b3, wcat, lru_lambda, n_ctx=n_ctx)
        xc, h2, aff = _merge_call(l, xc, att, y, gz, sa, sr, modcat, g_ffn, wab_bf, wrb_bf, wo_bf, wr_hi, wr_lo,
                                  n_ctx=n_ctx, tt=tt, with_ctx=with_ctx)
        ctx_rows = n_ctx if with_ctx else 0
        code, aff_rows, bounds = _route_call(aff, n_ctx=ctx_rows, tt=tt)
        n_slots = cap_lat + (cap_ctx if with_ctx else 0)
        ye = _moe_call(l, code, aff_rows, h2, w_gate, w_up, w_down, n_slots=n_slots, n_ctx=ctx_rows)
        xc = _combine_call(l, xc, code, bounds, ye, modcat, final_norm_g[None, :], n_ctx=ctx_rows, tt=tt, final=last)
    return xc
```

```python
import functools
import math

import jax
import jax.numpy as jnp
from jax import lax
from jax.experimental import pallas as pl
from jax.experimental.pallas import tpu as pltpu

F32 = jnp.float32
BF16 = jnp.bfloat16
I32 = jnp.int32

HEAD_DIM = 128
N_HEADS = 8
N_KV_HEADS = 2
GROUP = N_HEADS // N_KV_HEADS
ATT_BLOCK = 128
GRID_W = 64
ROPE_THETA = 10000.0
LRU_BLOCKS = 8
LRU_C = 8.0
CONV_W = 4
CONV_LEFT = 2
N_EXPERTS = 16
EC_FACTOR = 2
N_MOD = 6
EPS = 1e-6
NEG_BIG = -1e30
LOG2E = 1.4426950408889634
QK_SCALE = HEAD_DIM ** -0.5 * LOG2E

LANES = 128
SUBLANES = 8
SCAN_SEGMENTS = 64
INPROJ_MAX_ROWS = 768
ATT_BLOCKS_PER_STEP = 2
COMBINE_WINDOW = 80
MIN_NORMAL = 2.0 ** -126
MANTISSA_STEPS = 32
VMEM_LIMIT = 56 * 1024 * 1024


def _sigmoid(x):
    return 0.5 * jnp.tanh(0.5 * x) + 0.5


def _gelu_tanh(x):
    c = math.sqrt(2.0 / math.pi)
    return 0.5 * x * (1.0 + jnp.tanh(c * (x + 0.044715 * (x * x * x))))


def _rmsnorm(x, g):
    ms = jnp.mean(x * x, axis=-1, keepdims=True)
    return x * lax.rsqrt(ms + EPS) * g


def _params(sem):
    return pltpu.CompilerParams(dimension_semantics=sem, vmem_limit_bytes=VMEM_LIMIT)


def _ada_body(c_ref, w_ref, b_ref, o_ref):
    c = c_ref[...]
    act = c * _sigmoid(c)
    o_ref[...] = jnp.dot(act, w_ref[...], preferred_element_type=F32,
                         precision=lax.Precision.HIGHEST) + b_ref[...]


def _ada_call(cc, ada_w, ada_b):
    depth, d, n6 = ada_w.shape
    rows = cc.shape[0]
    tn = 1024
    return pl.pallas_call(
        _ada_body,
        grid=(depth, n6 // tn),
        in_specs=[pl.BlockSpec((rows, d), lambda l, j: (0, 0)),
                  pl.BlockSpec((None, d, tn), lambda l, j: (l, 0, j)),
                  pl.BlockSpec((None, 1, tn), lambda l, j: (l, 0, j))],
        out_specs=pl.BlockSpec((None, rows, tn), lambda l, j: (l, 0, j)),
        out_shape=jax.ShapeDtypeStruct((depth, rows, n6), F32),
        compiler_params=_params(("parallel", "parallel")),
        name="ada_mod",
    )(cc, ada_w, ada_b.reshape(depth, 1, n6))


def _rope(xh, cos, ssin, hi):
    sw = jnp.where(hi, pltpu.roll(xh, 32, 1), pltpu.roll(xh, LANES - 32, 1))
    return xh * cos + sw * ssin


def _inproj_body(*refs, d, n_ctx, parts, split_src):
    n_in = parts * (2 if split_src else 1)
    x_parts = refs[:n_in]
    mod_ref, g_ref, w_ref, cos_ref, sin_ref = refs[n_in:n_in + 5]
    q_ref, k_ref, v_ref, u_ref, z_ref, ma_ref, mr_ref = refs[n_in + 5:]
    tp = x_parts[0].shape[0]
    tt = tp * parts
    qw = N_HEADS * HEAD_DIM
    kw = N_KV_HEADS * HEAD_DIM
    for part in range(parts):
        r0 = part * tp
        rows = slice(r0, r0 + tp)
        if split_src:
            from_ctx = pl.program_id(1) * tt + r0 < n_ctx
            x = jnp.where(from_ctx, x_parts[part][...], x_parts[parts + part][...])
        else:
            x = x_parts[part][...]
        is_ctx = pl.program_id(1) * tt + r0 + lax.broadcasted_iota(I32, (tp, 1), 0) < n_ctx
        shift = jnp.where(is_ctx, mod_ref[0, 0:1, :], mod_ref[1, 0:1, :])
        scale = jnp.where(is_ctx, mod_ref[0, 1:2, :], mod_ref[1, 1:2, :])
        h = _rmsnorm(x, g_ref[...]) * (1.0 + scale) + shift
        hb = h.astype(BF16)
        cos = cos_ref[rows, :]
        ssin = sin_ref[rows, :]
        cos_q = cos * QK_SCALE
        ssin_q = ssin * QK_SCALE
        hi = (lax.broadcasted_iota(I32, cos.shape, 1) & 32) != 0

        def proj(lo, width, hb=hb):
            return jnp.dot(hb, w_ref[:, lo:lo + width], preferred_element_type=F32)

        for h0 in range(0, N_HEADS, 2):
            qq = proj(h0 * HEAD_DIM, 2 * HEAD_DIM)
            for j in range(2):
                q_ref[rows, (h0 + j) * HEAD_DIM:(h0 + j + 1) * HEAD_DIM] = _rope(
                    qq[:, j * HEAD_DIM:(j + 1) * HEAD_DIM], cos_q, ssin_q, hi).astype(BF16)
        for h0 in range(0, N_KV_HEADS, 2):
            kk = proj(qw + h0 * HEAD_DIM, 2 * HEAD_DIM)
            for j in range(2):
                k_ref[rows, (h0 + j) * HEAD_DIM:(h0 + j + 1) * HEAD_DIM] = _rope(
                    kk[:, j * HEAD_DIM:(j + 1) * HEAD_DIM], cos, ssin, hi).astype(BF16)
        off = qw + kw
        v_ref[rows, :] = proj(off, kw).astype(BF16)
        off += kw
        u_ref[rows, :] = proj(off, d)
        off += d
        for c0 in range(0, d, 2 * LANES):
            z_ref[rows, c0:c0 + 2 * LANES] = _gelu_tanh(proj(off + c0, 2 * LANES)).astype(BF16)
        off += d
        for c0 in range(0, d, 2 * LANES):
            ma_ref[rows, c0:c0 + 2 * LANES] = _sigmoid(proj(off + c0, 2 * LANES)).astype(BF16)
        off += d
        for c0 in range(0, d, 2 * LANES):
            mr_ref[rows, c0:c0 + 2 * LANES] = _sigmoid(proj(off + c0, 2 * LANES)).astype(BF16)


def _inproj_call(l, srcs, modcat, g, w_in_bf, cos_t, sin_t, *, n_ctx, tp):
    b, d = srcs[0].shape[0], srcs[0].shape[2]
    t = sum(s.shape[1] for s in srcs)
    split_src = len(srcs) == 2
    in_w = w_in_bf.shape[2]
    parts = max(c for c in range(1, INPROJ_MAX_ROWS // tp + 1) if (t // tp) % c == 0)
    tt = parts * tp
    nct = n_ctx // tp
    qw = N_HEADS * HEAD_DIM
    kw = N_KV_HEADS * HEAD_DIM
    row = lambda bi, ti: (bi, ti, 0)
    outs = [jax.ShapeDtypeStruct((b, t, qw), BF16), jax.ShapeDtypeStruct((b, t, kw), BF16),
            jax.ShapeDtypeStruct((b, t, kw), BF16), jax.ShapeDtypeStruct((b, t, d), F32),
            jax.ShapeDtypeStruct((b, t, d), BF16), jax.ShapeDtypeStruct((b, t, d), BF16),
            jax.ShapeDtypeStruct((b, t, d), BF16)]
    part_spec = lambda f: pl.BlockSpec((None, tp, d), f)
    if split_src:
        x_specs = ([part_spec(lambda bi, ti, r=r: (bi, jnp.minimum(ti * parts + r, nct - 1), 0)) for r in range(parts)]
                   + [part_spec(lambda bi, ti, r=r: (bi, jnp.maximum(ti * parts + r - nct, 0), 0))
                      for r in range(parts)])
        x_args = [srcs[0]] * parts + [srcs[1]] * parts
    else:
        x_specs = [part_spec(lambda bi, ti, r=r: (bi, ti * parts + r, 0)) for r in range(parts)]
        x_args = [srcs[0]] * parts
    return pl.pallas_call(
        functools.partial(_inproj_body, d=d, n_ctx=n_ctx, parts=parts, split_src=split_src),
        grid=(b, t // tt),
        in_specs=x_specs + [
                  pl.BlockSpec((None, None, 2, N_MOD, d), lambda bi, ti: (l, bi, 0, 0, 0)),
                  pl.BlockSpec((None, 1, d), lambda bi, ti: (l, 0, 0)),
                  pl.BlockSpec((None, d, in_w), lambda bi, ti: (l, 0, 0)),
                  pl.BlockSpec((tt, HEAD_DIM), lambda bi, ti: (ti, 0)),
                  pl.BlockSpec((tt, HEAD_DIM), lambda bi, ti: (ti, 0))],
        out_specs=[pl.BlockSpec((None, tt, s.shape[2]), row) for s in outs],
        out_shape=outs,
        compiler_params=_params(("parallel", "parallel")),
        name="in_proj",
    )(*x_args, modcat, g, w_in_bf, cos_t, sin_t)


def _attn_body(q_ref, k_ref, v_ref, sink_ref, o_ref, *, n_ctx, t_all, blk_off, qb):
    nc = n_ctx // ATT_BLOCK
    band = 3 * ATT_BLOCK
    nt = (((1,), (1,)), ((), ()))
    tn = (((0,), (0,)), ((), ()))
    for j in range(qb):
        n = pl.program_id(1) * qb + j + blk_off
        qrows = slice(j * ATT_BLOCK, (j + 1) * ATT_BLOCK)
        start = jnp.clip(n_ctx + (n - nc - 1) * ATT_BLOCK, 0, t_all - band)
        start = pl.multiple_of(start, ATT_BLOCK)

        kpos = start - n_ctx + lax.broadcasted_iota(I32, (band, ATT_BLOCK), 0)
        qpos = (n - nc) * ATT_BLOCK + lax.broadcasted_iota(I32, (band, ATT_BLOCK), 1)
        valid = (kpos >= 0) & (jnp.abs(qpos - kpos) <= ATT_BLOCK) & (n >= nc)
        bias = jnp.where(valid, 0.0, NEG_BIG)
        bias = jnp.concatenate([bias] * GROUP, axis=1)

        for kh in range(N_KV_HEADS):
            lanes = slice(kh * HEAD_DIM, (kh + 1) * HEAD_DIM)
            qs = jnp.concatenate(
                [q_ref[qrows, (kh * GROUP + g) * HEAD_DIM:(kh * GROUP + g + 1) * HEAD_DIM] for g in range(GROUP)],
                axis=0)
            kb = k_ref[pl.ds(start, band), lanes]
            vb = v_ref[pl.ds(start, band), lanes]
            kc = k_ref[0:n_ctx, lanes]
            vc = v_ref[0:n_ctx, lanes]
            s_band = lax.dot_general(kb, qs, nt, preferred_element_type=F32) + bias
            s_ctx = lax.dot_general(kc, qs, nt, preferred_element_type=F32)
            sink = jnp.concatenate(
                [sink_ref[kh * GROUP + g:kh * GROUP + g + 1, :] for g in range(GROUP)], axis=1) * LOG2E
            m = jnp.maximum(sink, jnp.maximum(jnp.max(s_band, axis=0, keepdims=True),
                                              jnp.max(s_ctx, axis=0, keepdims=True)))
            p_band = jnp.exp2(s_band - m)
            p_ctx = jnp.exp2(s_ctx - m)
            denom = (jnp.exp2(sink - m) + jnp.sum(p_band, axis=0, keepdims=True)
                     + jnp.sum(p_ctx, axis=0, keepdims=True))
            o_t = (lax.dot_general(vb, p_band.astype(BF16), tn, preferred_element_type=F32)
                   + lax.dot_general(vc, p_ctx.astype(BF16), tn, preferred_element_type=F32)) * (1.0 / denom)
            for g in range(GROUP):
                hh = kh * GROUP + g
                o_ref[qrows, hh * HEAD_DIM:(hh + 1) * HEAD_DIM] = (
                    o_t[:, g * ATT_BLOCK:(g + 1) * ATT_BLOCK].T.astype(BF16))


def _attn_call(l, q, k, v, sink_rows, *, n_ctx, with_ctx):
    b, t, qw = q.shape
    kw = k.shape[2]
    blk_off = 0 if with_ctx else n_ctx // ATT_BLOCK
    nblk = t // ATT_BLOCK - blk_off
    qb = ATT_BLOCKS_PER_STEP if nblk % ATT_BLOCKS_PER_STEP == 0 and blk_off % ATT_BLOCKS_PER_STEP == 0 else 1
    rows = qb * ATT_BLOCK
    return pl.pallas_call(
        functools.partial(_attn_body, n_ctx=n_ctx, t_all=t, blk_off=blk_off, qb=qb),
        grid=(b, nblk // qb),
        in_specs=[pl.BlockSpec((None, rows, qw), lambda bi, ni: (bi, ni + blk_off // qb, 0)),
                  pl.BlockSpec((None, t, kw), lambda bi, ni: (bi, 0, 0)),
                  pl.BlockSpec((None, t, kw), lambda bi, ni: (bi, 0, 0)),
                  pl.BlockSpec((None, N_HEADS, LANES), lambda bi, ni: (l, 0, 0))],
        out_specs=pl.BlockSpec((None, rows, qw), lambda bi, ni: (bi, ni, 0)),
        out_shape=jax.ShapeDtypeStruct((b, nblk * ATT_BLOCK, qw), BF16),
        compiler_params=_params(("parallel", "arbitrary")),
        name="win_attn",
    )(q, k, v, sink_rows)


def _lru_body(u_ref, cw_ref, cb_ref, w_ref, lam_ref, y_ref,
              p_f, a_b, b_b, h_f, h_b, y32, *, n_ctx, n_lat):
    t_all = n_ctx + n_lat
    nseg = SCAN_SEGMENTS
    seg = t_all // nseg

    lam = lam_ref[...]
    nl = -lam
    sp = jnp.maximum(nl, 0.0) + jnp.log(1.0 + jnp.exp(-jnp.abs(nl)))
    c2 = sp * (-0.5 * LRU_C * LOG2E)
    cw = cw_ref[...]
    cb = cb_ref[...]
    wcat = w_ref[...]
    lane = lax.broadcasted_iota(I32, (SCAN_SEGMENTS, LANES), 1)
    bias_cols = jnp.where(lane < 2, 1.0, 0.0).astype(BF16)

    def rows(i):
        return pl.ds(i, nseg, stride=seg)

    seg_id = lax.broadcasted_iota(I32, (nseg, LANES), 0)
    ones = jnp.ones((nseg, LANES), F32)
    zeros = jnp.zeros((nseg, LANES), F32)

    def same_part(r, q):
        return (r < n_ctx) == (q < n_ctx)

    def tap(i, off):
        j = i + off
        if 0 <= j < seg:
            v = u_ref[rows(j), :]
        elif j < 0:
            v = pltpu.roll(u_ref[rows(j + seg), :], 1, 0)
        else:
            v = pltpu.roll(u_ref[rows(j - seg), :], nseg - 1, 0)
        for s in range(nseg):
            r = i + seg * s
            if not (0 <= r + off < t_all and same_part(r, r + off)):
                v = jnp.where(seg_id == s, 0.0, v)
        return v

    brk_s, brk_i = (n_ctx - 1) // seg, (n_ctx - 1) % seg

    hf_fin, pf_tot = zeros, ones
    for i in range(seg):
        uc = cb
        for kk in range(CONV_W):
            uc = uc + cw[kk:kk + 1, :] * tap(i, kk - CONV_LEFT)
        lhs = jnp.concatenate([uc.astype(BF16), bias_cols], axis=1)
        th = jnp.tanh(jnp.dot(lhs, wcat, preferred_element_type=F32))
        uh = 0.5 * uc
        ab_dir = []
        for dr in range(2):
            t_r = th[:, (2 * dr) * LANES:(2 * dr + 1) * LANES]
            t_i = th[:, (2 * dr + 1) * LANES:(2 * dr + 2) * LANES]
            cc = c2[dr:dr + 1, :]
            a = jnp.exp2(t_r * cc + cc)
            om = 1.0 - a * a
            root = jnp.where(om > 0.0, om * lax.rsqrt(om), 0.0)
            ab_dir.append((a, root * (t_i * uh + uh)))
        af, bf = ab_dir[0]
        hf_fin = af * hf_fin + bf
        pf_tot = af * pf_tot
        p_f[rows(i), :] = pf_tot
        h_f[rows(i), :] = hf_fin
        ab, bb = ab_dir[1]
        if i == brk_i:
            ab = jnp.where(seg_id == brk_s, 0.0, ab)
        a_b[rows(i), :] = ab
        b_b[rows(i), :] = bb

    hin = jnp.zeros((1, LANES), F32)
    hin_f = []
    for s in range(nseg):
        hin_f.append(hin)
        hin = hf_fin[s:s + 1, :] + pf_tot[s:s + 1, :] * hin
    hin_f = jnp.concatenate(hin_f, axis=0)

    hb_fin, pb_tot = zeros, ones
    for i in range(seg):
        j = seg - 1 - i
        ab = a_b[rows(j), :]
        hb_fin = ab * hb_fin + b_b[rows(j), :]
        pb_tot = ab * pb_tot
        a_b[rows(j), :] = pb_tot
        h_b[rows(j), :] = hb_fin
        h_f[rows(i), :] = h_f[rows(i), :] + p_f[rows(i), :] * hin_f

    order = [(brk_s - k) % nseg for k in range(nseg)]
    hin_b = [None] * nseg
    hin = jnp.zeros((1, LANES), F32)
    for s in order:
        hin_b[s] = hin
        hin = hb_fin[s:s + 1, :] + pb_tot[s:s + 1, :] * hin
    hin_b[brk_s] = hin
    hin_b = jnp.concatenate(hin_b, axis=0)

    for j in range(seg):
        y32[rows(j), :] = h_f[rows(j), :] + (h_b[rows(j), :] + a_b[rows(j), :] * hin_b)
    y_ref[...] = y32[...].astype(BF16)


def _lru_call(l, u, conv_w, conv_b, wcat, lam, *, n_ctx):
    b, t, w = u.shape
    n_lat = t - n_ctx
    assert t % SCAN_SEGMENTS == 0 and w == LRU_BLOCKS * LANES and n_ctx > 0
    scr = [pltpu.VMEM((t, LANES), F32) for _ in range(6)]
    return pl.pallas_call(
        functools.partial(_lru_body, n_ctx=n_ctx, n_lat=n_lat),
        grid=(b, LRU_BLOCKS),
        in_specs=[pl.BlockSpec((None, t, LANES), lambda bi, ci: (bi, 0, ci)),
                  pl.BlockSpec((None, CONV_W, LANES), lambda bi, ci: (l, 0, ci)),
                  pl.BlockSpec((None, 1, LANES), lambda bi, ci: (l, 0, ci)),
                  pl.BlockSpec((None, None, 2 * LANES, 4 * LANES), lambda bi, ci: (l, ci, 0, 0)),
                  pl.BlockSpec((None, 2, LANES), lambda bi, ci: (l, 0, ci))],
        out_specs=pl.BlockSpec((None, t, LANES), lambda bi, ci: (bi, 0, ci)),
        out_shape=jax.ShapeDtypeStruct((b, t, w), BF16),
        scratch_shapes=scr,
        compiler_params=_params(("parallel", "parallel")),
        name="rg_lru",
    )(u, conv_w, conv_b, wcat, lam)


def _merge_body(*refs, n_ctx, split_src):
    if split_src:
        xctx_ref, x_ref = refs[0], refs[1]
        refs = refs[1:]
    (x_ref, att_ref, y_ref, gz_ref, sa_ref, sr_ref, mod_ref, g_ref,
     wab_ref, wrb_ref, wo_ref, wrh_ref, wrl_ref, xo_ref, h2_ref, aff_ref, mix_s) = refs
    tt, d = x_ref.shape
    from_ctx = pl.program_id(1) * tt < n_ctx

    def x_in(cols):
        if split_src:
            return jnp.where(from_ctx, xctx_ref[:, cols], x_ref[:, cols])
        return x_ref[:, cols]

    ch = 2 * LANES
    n_exp = aff_ref.shape[0]
    att = att_ref[...]
    rec_in = y_ref[...] * gz_ref[...]
    for c0 in range(0, d, ch):
        cols = slice(c0, c0 + ch)
        att_d = jnp.dot(att, wab_ref[:, cols], preferred_element_type=F32)
        rec_d = jnp.dot(rec_in, wrb_ref[:, cols], preferred_element_type=F32)
        mix = sa_ref[:, cols].astype(F32) * att_d + sr_ref[:, cols].astype(F32) * rec_d
        mix_s[:, cols] = mix.astype(BF16)
    mix = mix_s[...]
    part = jnp.zeros((tt, LANES), F32)
    for c0 in range(0, d, ch):
        cols = slice(c0, c0 + ch)
        out = jnp.dot(mix, wo_ref[:, cols], preferred_element_type=F32)
        xn = x_in(cols) + mod_ref[2:3, cols] * out
        xo_ref[:, cols] = xn
        sq = xn * xn
        part = part + sq[:, 0:LANES] + sq[:, LANES:ch]
    rs = lax.rsqrt(jnp.sum(part, axis=-1, keepdims=True) * (1.0 / d) + EPS)
    logits = jnp.zeros((tt, LANES), F32)
    for c0 in range(0, d, ch):
        cols = slice(c0, c0 + ch)
        h2 = xo_ref[:, cols] * rs * g_ref[:, cols] * (1.0 + mod_ref[4:5, cols]) + mod_ref[3:4, cols]
        hi = h2.astype(BF16)
        h2_ref[:, cols] = hi
        lo = (h2 - hi.astype(F32)).astype(BF16)
        logits = (logits + jnp.dot(hi, wrh_ref[cols, :], preferred_element_type=F32)
                  + jnp.dot(lo, wrh_ref[cols, :], preferred_element_type=F32)
                  + jnp.dot(hi, wrl_ref[cols, :], preferred_element_type=F32))
    lt = logits.T[0:n_exp, :]
    mx = jnp.max(lt, axis=0, keepdims=True)
    ex = jnp.exp(lt - mx)
    aff_ref[...] = ex / jnp.sum(ex, axis=0, keepdims=True)


def _merge_call(l, srcs, att, y, gz, sa, sr, modcat, g, wab, wrb, wo, wr_hi, wr_lo, *, n_ctx, tt, with_ctx):
    b, d = srcs[0].shape[0], srcs[0].shape[2]
    t = sum(s.shape[1] for s in srcs)
    split_src = len(srcs) == 2
    assert with_ctx or not split_src
    nct = n_ctx // tt
    off = 0 if with_ctx else nct
    t_out = t - off * tt
    row = lambda bi, ti: (bi, ti + off, 0)
    orow = lambda bi, ti: (bi, ti, 0)
    lay = lambda bi, ti: (l, 0, 0)
    if split_src:
        x_specs = [pl.BlockSpec((None, tt, d), lambda bi, ti: (bi, jnp.minimum(ti, nct - 1), 0)),
                   pl.BlockSpec((None, tt, d), lambda bi, ti: (bi, jnp.maximum(ti - nct, 0), 0))]
    else:
        x_specs = [pl.BlockSpec((None, tt, d), row)]
    return pl.pallas_call(
        functools.partial(_merge_body, n_ctx=n_ctx, split_src=split_src),
        grid=(b, t // tt - off),
        in_specs=x_specs + [pl.BlockSpec((None, tt, d), orow)]
        + [pl.BlockSpec((None, tt, d), row)] * 4 + [
            pl.BlockSpec((None, None, None, N_MOD, d),
                         lambda bi, ti: (l, bi, jnp.where(ti + off >= nct, 1, 0), 0, 0)),
            pl.BlockSpec((None, 1, d), lay),
            pl.BlockSpec((None,) + wab.shape[1:], lay), pl.BlockSpec((None,) + wrb.shape[1:], lay),
            pl.BlockSpec((None,) + wo.shape[1:], lay),
            pl.BlockSpec((None, d, LANES), lay), pl.BlockSpec((None, d, LANES), lay)],
        out_specs=[pl.BlockSpec((None, tt, d), orow), pl.BlockSpec((None, tt, d), orow),
                   pl.BlockSpec((None, N_EXPERTS, tt), lambda bi, ti: (bi, 0, ti))],
        out_shape=[jax.ShapeDtypeStruct((b, t_out, d), F32), jax.ShapeDtypeStruct((b, t_out, d), BF16),
                   jax.ShapeDtypeStruct((b, N_EXPERTS, t_out), F32)],
        scratch_shapes=[pltpu.VMEM((tt, d), BF16)],
        compiler_params=_params(("parallel", "parallel")),
        name="merge_router",
    )(*srcs, att, y, gz, sa, sr, modcat, g, wab, wrb, wo, wr_hi, wr_lo)


def _prefix_excl(mask, tri):
    rows, length = mask.shape
    run = jnp.zeros((rows, 1), F32)
    parts = []
    for kb in range(length // LANES):
        blk = mask[:, kb * LANES:(kb + 1) * LANES]
        loc = jnp.dot(blk.astype(BF16), tri, preferred_element_type=F32)
        parts.append(loc + run)
        run = run + jnp.sum(blk, axis=1, keepdims=True)
    return jnp.concatenate(parts, axis=1)


def _count_ge(aff, cand):
    return jnp.sum(jnp.where(aff >= cand, 1.0, 0.0), axis=1, keepdims=True)


def _select_topcap(segments, tri):
    n = len(segments)
    affs = [s[0] for s in segments]
    caps = [float(s[1]) for s in segments]
    rows = affs[0].shape[0]
    base = jnp.full((rows, 1), MIN_NORMAL, F32)
    has = [_count_ge(affs[i], base) >= caps[i] for i in range(n)]

    def best(i, cur, cands):
        for c in cands:
            cur = jnp.where(_count_ge(affs[i], c) >= caps[i], c, cur)
        return cur

    lo = [base] * n
    for j in range(6, -1, -2):
        fa = float(2.0 ** (2 ** j))
        fb = float(2.0 ** (2 ** (j - 1))) if j > 0 else None
        for i in range(n):
            mults = [fa] if fb is None else [fb, fa, fa * fb]
            lo[i] = best(i, lo[i], [lo[i] * m for m in mults])
    lo = [jnp.where(has[i], lo[i], 0.0) for i in range(n)]
    cur = list(lo)
    for j in range(1, MANTISSA_STEPS + 1, 2):
        q = float(2.0 ** -(j + 1))
        for i in range(n):
            cur[i] = best(i, cur[i], [cur[i] + lo[i] * (k * q) for k in (1, 2, 3)])
    codes = []
    for i in range(n):
        aff = affs[i]
        nxt = jnp.where(has[i], cur[i] + lo[i] * float(2.0 ** -23), base)
        ge = jnp.where(aff >= cur[i], 1.0, 0.0)
        tie = ge * jnp.where(aff < nxt, 1.0, 0.0)
        excess = jnp.sum(ge, axis=1, keepdims=True) - caps[i]
        after = jnp.sum(tie, axis=1, keepdims=True) - (_prefix_excl(tie, tri) + tie)
        sel = ge - tie * jnp.where(after < excess, 1.0, 0.0)
        pos = _prefix_excl(sel, tri)
        codes.append(jnp.where(sel > 0.5, pos, -1.0))
    return codes


def _route_body(aff_ref, code_ref, affrow_ref, bnd_ref, *, n_ctx, n_lat, tt):
    aff = aff_ref[...]
    e = aff.shape[0]
    r = lax.broadcasted_iota(I32, (LANES, LANES), 0)
    c = lax.broadcasted_iota(I32, (LANES, LANES), 1)
    tri = (r < c).astype(BF16)
    cap_lat = EC_FACTOR * n_lat // N_EXPERTS
    if n_ctx:
        cap_ctx = EC_FACTOR * n_ctx // N_EXPERTS
        code_ctx, code = _select_topcap([(aff[:, 0:n_ctx], cap_ctx), (aff[:, n_ctx:n_ctx + n_lat], cap_lat)], tri)
        code = jnp.concatenate([code_ctx, jnp.where(code >= 0.0, code + cap_ctx, -1.0)], axis=1)
    else:
        (code,) = _select_topcap([(aff, cap_lat)], tri)
    taken = jnp.where(code >= 0.0, 1.0, 0.0)
    run = jnp.zeros((e, 1), F32)
    bnds = [run]
    for t0 in range(0, n_ctx + n_lat, tt):
        run = run + jnp.sum(taken[:, t0:t0 + tt], axis=1, keepdims=True)
        bnds.append(run)
    bnd_ref[...] = jnp.concatenate(bnds, axis=1).astype(I32)
    code = code.astype(I32)
    for ei in range(e):
        code_ref[ei] = code[ei:ei + 1, :]
        affrow_ref[ei] = aff[ei:ei + 1, :]


def _route_call(aff, *, n_ctx, tt):
    b, e, t = aff.shape
    spec4 = pl.BlockSpec((None, e, 1, t), lambda bi: (bi, 0, 0, 0))
    nb = t // tt + 1
    return pl.pallas_call(
        functools.partial(_route_body, n_ctx=n_ctx, n_lat=t - n_ctx, tt=tt),
        grid=(b,),
        in_specs=[pl.BlockSpec((None, e, t), lambda bi: (bi, 0, 0))],
        out_specs=[spec4, spec4, pl.BlockSpec((None, e, nb), lambda bi: (bi, 0, 0))],
        out_shape=[jax.ShapeDtypeStruct((b, e, 1, t), I32), jax.ShapeDtypeStruct((b, e, 1, t), F32),
                   jax.ShapeDtypeStruct((b, e, nb), I32)],
        compiler_params=_params(("parallel",)),
        name="route_select",
    )(aff)


def _moe_body(code_ref, aff_ref, h_ref, wg32_ref, wu32_ref, wd32_ref, ye_ref, wg_s, wu_s, wd_s,
              *, n_ctx, n_ctx_slots):
    p = pl.program_id(0)
    bi = pl.program_id(1)
    fill = p % 2
    use = 1 - fill

    def store_weight_slice():
        rd = wg32_ref.shape[0]
        rf = wd32_ref.shape[0]
        r0 = pl.multiple_of(bi * rd, rd)
        f0 = pl.multiple_of(bi * rf, rf)
        wg_s[fill, pl.ds(r0, rd), :] = wg32_ref[...].astype(BF16)
        wu_s[fill, pl.ds(r0, rd), :] = wu32_ref[...].astype(BF16)
        wd_s[fill, pl.ds(f0, rf), :] = wd32_ref[...].astype(BF16)

    @pl.when(p == 0)
    def _():
        store_weight_slice()
        ye_ref[...] = jnp.zeros(ye_ref.shape, ye_ref.dtype)

    @pl.when(p > 0)
    def _():
        store_weight_slice()
        n_slots = ye_ref.shape[0]
        t = code_ref.shape[1]
        xs, gs = [], []
        for (s0, s1, t0, t1) in ((0, n_ctx_slots, 0, n_ctx), (n_ctx_slots, n_slots, n_ctx, t)):
            if s1 == s0:
                continue
            hit = code_ref[:, t0:t1] == s0 + lax.broadcasted_iota(I32, (s1 - s0, t1 - t0), 0)
            sel = jnp.where(hit, 1.0, 0.0).astype(BF16)
            xs.append(jnp.dot(sel, h_ref[t0:t1, :], preferred_element_type=F32).astype(BF16))
            gs.append(jnp.sum(jnp.where(hit, aff_ref[:, t0:t1], 0.0), axis=1, keepdims=True))
        xe = jnp.concatenate(xs, axis=0)
        gcol = jnp.concatenate(gs, axis=0)
        gate = jnp.dot(xe, wg_s[use], preferred_element_type=F32)
        up = jnp.dot(xe, wu_s[use], preferred_element_type=F32)
        hid = (gate * _sigmoid(gate) * up).astype(BF16)
        ye = jnp.dot(hid, wd_s[use], preferred_element_type=F32) * gcol
        ye_ref[...] = ye.astype(BF16)


def _moe_call(l, code, aff4, h2, wg, wu, wd, *, n_slots, n_ctx):
    b, t, d = h2.shape
    _, e, _, f = wg.shape
    assert d % (b * 16) == 0 and f % (b * 16) == 0
    rd, rf = d // b, f // b
    n_ctx_slots = EC_FACTOR * n_ctx // N_EXPERTS
    ex = lambda pi: jnp.maximum(pi - 1, 0)
    wmap = lambda pi, bi: (l, jnp.minimum(pi, e - 1), jnp.where(pi < e, bi, b - 1), 0)
    return pl.pallas_call(
        functools.partial(_moe_body, n_ctx=n_ctx, n_ctx_slots=n_ctx_slots),
        grid=(e + 1, b),
        in_specs=[pl.BlockSpec((None, None, 1, t), lambda pi, bi: (bi, ex(pi), 0, 0)),
                  pl.BlockSpec((None, None, 1, t), lambda pi, bi: (bi, ex(pi), 0, 0)),
                  pl.BlockSpec((None, t, d), lambda pi, bi: (jnp.where(pi == 0, 0, bi), 0, 0)),
                  pl.BlockSpec((None, None, rd, f), wmap),
                  pl.BlockSpec((None, None, rd, f), wmap),
                  pl.BlockSpec((None, None, rf, d), wmap)],
        out_specs=pl.BlockSpec((None, None, n_slots, d), lambda pi, bi: (bi, jnp.where(pi == 0, e, pi - 1), 0, 0)),
        out_shape=jax.ShapeDtypeStruct((b, e + 1, n_slots, d), BF16),
        scratch_shapes=[pltpu.VMEM((2, d, f), BF16), pltpu.VMEM((2, d, f), BF16), pltpu.VMEM((2, f, d), BF16)],
        compiler_params=_params(("arbitrary", "arbitrary")),
        name="moe_ffn",
    )(code, aff4, h2, wg, wu, wd)


def _combine_body(bnd_ref, x_ref, code_ref, ye_ref, mod_ref, gf_ref, o_ref, *, n_slots, final, win):
    bi = pl.program_id(0)
    ti = pl.program_id(1)
    nb = pl.num_programs(1) + 1
    e = ye_ref.shape[0]
    tt, d = x_ref.shape
    tn = (((0,), (0,)), ((), ()))
    starts = []
    fits = None
    for ei in range(e):
        at = (bi * e + ei) * nb + ti
        lo = bnd_ref[at]
        hi = bnd_ref[at + 1]
        st = jnp.minimum(lax.shift_left(lax.shift_right_logical(lo, 4), 4), n_slots - win)
        starts.append(pl.multiple_of(st, 16))
        ok = hi - st <= win
        fits = ok if fits is None else jnp.logical_and(fits, ok)

    def finish(acc):
        xn = x_ref[...] + mod_ref[5:6, :] * acc
        if final:
            xn = _rmsnorm(xn, gf_ref[...])
        o_ref[...] = xn

    @pl.when(fits)
    def _():
        slot = lax.broadcasted_iota(I32, (win, tt), 0)
        sel = jnp.concatenate(
            [jnp.where(code_ref[ei] - starts[ei] == slot, 1.0, 0.0).astype(BF16) for ei in range(e)], axis=0)
        ye = jnp.concatenate([ye_ref[ei, pl.ds(starts[ei], win), :] for ei in range(e)], axis=0)
        finish(lax.dot_general(sel, ye, tn, preferred_element_type=F32))

    @pl.when(jnp.logical_not(fits))
    def _():
        slot = lax.broadcasted_iota(I32, (n_slots, tt), 0)
        sel = jnp.concatenate(
            [jnp.where(code_ref[ei] == slot, 1.0, 0.0).astype(BF16) for ei in range(e)], axis=0)
        ye = ye_ref[...].reshape(e * n_slots, d)
        finish(lax.dot_general(sel, ye, tn, preferred_element_type=F32))


def _combine_call(l, xc, code, bounds, ye, modcat, gfinal, *, n_ctx, tt, final):
    b, t, d = xc.shape
    e, n_slots = N_EXPERTS, ye.shape[2]
    nct = n_ctx // tt
    win = min(COMBINE_WINDOW, n_slots)
    grid_spec = pltpu.PrefetchScalarGridSpec(
        num_scalar_prefetch=1,
        grid=(b, t // tt),
        in_specs=[pl.BlockSpec((None, tt, d), lambda bi, ti, bnd: (bi, ti, 0)),
                  pl.BlockSpec((None, e, 1, tt), lambda bi, ti, bnd: (bi, 0, 0, ti)),
                  pl.BlockSpec((None, e, n_slots, d), lambda bi, ti, bnd: (bi, 0, 0, 0)),
                  pl.BlockSpec((None, None, None, N_MOD, d),
                               lambda bi, ti, bnd: (l, bi, jnp.where(ti >= nct, 1, 0), 0, 0)),
                  pl.BlockSpec((1, d), lambda bi, ti, bnd: (0, 0))],
        out_specs=pl.BlockSpec((None, tt, d), lambda bi, ti, bnd: (bi, ti, 0)))
    return pl.pallas_call(
        functools.partial(_combine_body, n_slots=n_slots, final=final, win=win),
        grid_spec=grid_spec,
        out_shape=jax.ShapeDtypeStruct((b, t, d), F32),
        compiler_params=_params(("parallel", "arbitrary")),
        name="moe_combine",
    )(bounds.reshape(-1), xc, code, ye, modcat, gfinal)


def _rope_tables(n_ctx, n_lat):
    quarter = HEAD_DIM // 4
    t = jnp.arange(n_lat, dtype=jnp.int32)
    rows = (t // GRID_W).astype(F32)
    cols = (t % GRID_W).astype(F32)
    freqs = ROPE_THETA ** (-jnp.arange(quarter, dtype=F32) / quarter)
    ang_r = rows[:, None] * freqs[None, :]
    ang_c = cols[:, None] * freqs[None, :]
    cos = jnp.concatenate([jnp.cos(ang_r), jnp.cos(ang_r), jnp.cos(ang_c), jnp.cos(ang_c)], axis=1)
    sin = jnp.concatenate([-jnp.sin(ang_r), jnp.sin(ang_r), -jnp.sin(ang_c), jnp.sin(ang_c)], axis=1)
    cos = jnp.concatenate([jnp.ones((n_ctx, HEAD_DIM), F32), cos], axis=0)
    sin = jnp.concatenate([jnp.zeros((n_ctx, HEAD_DIM), F32), sin], axis=0)
    return cos, sin


def kernel(x, c, ctx, c_ctx, ada_w, ada_b, norm_mix_g, w_in, attn_sink, conv_w, conv_b, lru_wa, lru_ba, lru_wx,
           lru_bx, lru_lambda, w_attn_br, w_rec_br, w_out, norm_ffn_g, w_router, w_gate, w_up, w_down,
           final_norm_g):
    b, n_lat, d = x.shape
    n_ctx = ctx.shape[1]
    depth = ada_w.shape[0]
    tt = math.gcd(math.gcd(n_ctx, n_lat), 256)

    pad_rows = (-(b + 1)) % SUBLANES
    cc = jnp.concatenate([c, c_ctx[None, :], jnp.zeros((pad_rows, d), F32)], axis=0)
    mod = _ada_call(cc, ada_w, ada_b)
    mod_lat = mod[:, :b].reshape(depth, b, 1, N_MOD, d)
    mod_ctx = jnp.broadcast_to(mod[:, b].reshape(depth, 1, 1, N_MOD, d), (depth, b, 1, N_MOD, d))
    modcat = jnp.concatenate([mod_ctx, mod_lat], axis=2)

    cos_t, sin_t = _rope_tables(n_ctx, n_lat)
    srcs = (ctx, x) if depth > 1 else (jnp.concatenate([ctx, x], axis=1),)

    w_in_bf = w_in.astype(BF16)
    wab_bf = w_attn_br.astype(BF16)
    wrb_bf = w_rec_br.astype(BF16)
    wo_bf = w_out.astype(BF16)
    wcat = (0.5 * jnp.concatenate([lru_wa[:, 0], lru_wx[:, 0], lru_wa[:, 1], lru_wx[:, 1]], axis=-1)).astype(BF16)
    bcat = 0.5 * jnp.stack([lru_ba[:, 0], lru_bx[:, 0], lru_ba[:, 1], lru_bx[:, 1]], axis=1)
    bcat = bcat.reshape(depth, 4, LRU_BLOCKS, LANES).transpose(0, 2, 1, 3).reshape(depth, LRU_BLOCKS, 1, 4 * LANES)
    b_hi = bcat.astype(BF16)
    b_lo = (bcat - b_hi.astype(F32)).astype(BF16)
    wcat = jnp.concatenate(
        [wcat, b_hi, b_lo, jnp.zeros((depth, LRU_BLOCKS, LANES - 2, 4 * LANES), BF16)], axis=2)
    sink_rows = jnp.broadcast_to(attn_sink[:, :, None], (depth, N_HEADS, LANES))
    wr_pad = jnp.pad(w_router, ((0, 0), (0, 0), (0, LANES - w_router.shape[2])))
    wr_hi = wr_pad.astype(BF16)
    wr_lo = (wr_pad - wr_hi.astype(F32)).astype(BF16)
    g_mix = norm_mix_g[:, None, :]
    g_ffn = norm_ffn_g[:, None, :]
    conv_b3 = conv_b[:, None, :]

    cap_lat = EC_FACTOR * n_lat // N_EXPERTS
    cap_ctx = EC_FACTOR * n_ctx // N_EXPERTS
    for l in range(depth):
        last = l == depth - 1
        with_ctx = not last
        q, k, v, u, gz, sa, sr = _inproj_call(l, srcs, modcat, g_mix, w_in_bf, cos_t, sin_t, n_ctx=n_ctx, tp=tt)
        att = _attn_call(l, q, k, v, sink_rows, n_ctx=n_ctx, with_ctx=with_ctx)
        y = _lru_call(l, u, conv_w, conv_b3, wcat, lru_lambda, n_ctx=n_ctx)
        xc, h2, aff = _merge_call(l, srcs, att, y, gz, sa, sr, modcat, g_ffn, wab_bf, wrb_bf, wo_bf, wr_hi, wr_lo,
                                  n_ctx=n_ctx, tt=tt, with_ctx=with_ctx)
        ctx_rows = n_ctx if with_ctx else 0
        code, aff_rows, bounds = _route_call(aff, n_ctx=ctx_rows, tt=tt)
        n_slots = cap_lat + (cap_ctx if with_ctx else 0)
        ye = _moe_call(l, code, aff_rows, h2, w_gate, w_up, w_down, n_slots=n_slots, n_ctx=ctx_rows)
        xc = _combine_call(l, xc, code, bounds, ye, modcat, final_norm_g[None, :], n_ctx=ctx_rows, tt=tt, final=last)
        srcs = (xc,)
    return xc
```

```python
import functools
import math

import jax
import jax.numpy as jnp
from jax import lax
from jax.experimental import pallas as pl
from jax.experimental.pallas import tpu as pltpu

F32 = jnp.float32
BF16 = jnp.bfloat16
I32 = jnp.int32

HEAD_DIM = 128
N_HEADS = 8
N_KV_HEADS = 2
GROUP = N_HEADS // N_KV_HEADS
ATT_BLOCK = 128
GRID_W = 64
ROPE_THETA = 10000.0
LRU_BLOCKS = 8
LRU_C = 8.0
CONV_W = 4
CONV_LEFT = 2
N_EXPERTS = 16
EC_FACTOR = 2
N_MOD = 6
EPS = 1e-6
NEG_BIG = -1e30
LOG2E = 1.4426950408889634
QK_SCALE = HEAD_DIM ** -0.5 * LOG2E

LANES = 128
SUBLANES = 8
SCAN_SEGMENTS = 64
INPROJ_MAX_ROWS = 768
ATT_BLOCKS_PER_STEP = 2
COMBINE_WINDOW = 80
MIN_NORMAL = 2.0 ** -126
MANTISSA_STEPS = 32
VMEM_LIMIT = 56 * 1024 * 1024


def _sigmoid(x):
    return 0.5 * jnp.tanh(0.5 * x) + 0.5


def _gelu_tanh(x):
    c = math.sqrt(2.0 / math.pi)
    return 0.5 * x * (1.0 + jnp.tanh(c * (x + 0.044715 * (x * x * x))))


def _rmsnorm(x, g):
    ms = jnp.mean(x * x, axis=-1, keepdims=True)
    return x * lax.rsqrt(ms + EPS) * g


def _params(sem):
    return pltpu.CompilerParams(dimension_semantics=sem, vmem_limit_bytes=VMEM_LIMIT)


def _ada_body(c_ref, w_ref, b_ref, o_ref):
    c = c_ref[...]
    act = c * _sigmoid(c)
    o_ref[...] = jnp.dot(act, w_ref[...], preferred_element_type=F32,
                         precision=lax.Precision.HIGHEST) + b_ref[...]


def _ada_call(cc, ada_w, ada_b):
    depth, d, n6 = ada_w.shape
    rows = cc.shape[0]
    tn = 1024
    return pl.pallas_call(
        _ada_body,
        grid=(depth, n6 // tn),
        in_specs=[pl.BlockSpec((rows, d), lambda l, j: (0, 0)),
                  pl.BlockSpec((None, d, tn), lambda l, j: (l, 0, j)),
                  pl.BlockSpec((None, 1, tn), lambda l, j: (l, 0, j))],
        out_specs=pl.BlockSpec((None, rows, tn), lambda l, j: (l, 0, j)),
        out_shape=jax.ShapeDtypeStruct((depth, rows, n6), F32),
        compiler_params=_params(("parallel", "parallel")),
        name="ada_mod",
    )(cc, ada_w, ada_b.reshape(depth, 1, n6))


def _rope(xh, cos, ssin, hi):
    sw = jnp.where(hi, pltpu.roll(xh, 32, 1), pltpu.roll(xh, LANES - 32, 1))
    return xh * cos + sw * ssin


def _inproj_body(*refs, d, n_ctx, parts, split_src):
    n_in = parts * (2 if split_src else 1)
    x_parts = refs[:n_in]
    mod_ref, g_ref, w_ref, cos_ref, sin_ref = refs[n_in:n_in + 5]
    q_ref, k_ref, v_ref, u_ref, z_ref, ma_ref, mr_ref = refs[n_in + 5:]
    tp = x_parts[0].shape[0]
    tt = tp * parts
    qw = N_HEADS * HEAD_DIM
    kw = N_KV_HEADS * HEAD_DIM
    for part in range(parts):
        r0 = part * tp
        rows = slice(r0, r0 + tp)
        if split_src:
            from_ctx = pl.program_id(1) * tt + r0 < n_ctx
            x = jnp.where(from_ctx, x_parts[part][...], x_parts[parts + part][...])
        else:
            x = x_parts[part][...]
        is_ctx = pl.program_id(1) * tt + r0 + lax.broadcasted_iota(I32, (tp, 1), 0) < n_ctx
        shift = jnp.where(is_ctx, mod_ref[0, 0:1, :], mod_ref[1, 0:1, :])
        scale = jnp.where(is_ctx, mod_ref[0, 1:2, :], mod_ref[1, 1:2, :])
        h = _rmsnorm(x, g_ref[...]) * (1.0 + scale) + shift
        hb = h.astype(BF16)
        cos = cos_ref[rows, :]
        ssin = sin_ref[rows, :]
        cos_q = cos * QK_SCALE
        ssin_q = ssin * QK_SCALE
        hi = (lax.broadcasted_iota(I32, cos.shape, 1) & 32) != 0

        def proj(lo, width, hb=hb):
            return jnp.dot(hb, w_ref[:, lo:lo + width], preferred_element_type=F32)

        for h0 in range(0, N_HEADS, 2):
            qq = proj(h0 * HEAD_DIM, 2 * HEAD_DIM)
            for j in range(2):
                q_ref[rows, (h0 + j) * HEAD_DIM:(h0 + j + 1) * HEAD_DIM] = _rope(
                    qq[:, j * HEAD_DIM:(j + 1) * HEAD_DIM], cos_q, ssin_q, hi).astype(BF16)
        for h0 in range(0, N_KV_HEADS, 2):
            kk = proj(qw + h0 * HEAD_DIM, 2 * HEAD_DIM)
            for j in range(2):
                k_ref[rows, (h0 + j) * HEAD_DIM:(h0 + j + 1) * HEAD_DIM] = _rope(
                    kk[:, j * HEAD_DIM:(j + 1) * HEAD_DIM], cos, ssin, hi).astype(BF16)
        off = qw + kw
        v_ref[rows, :] = proj(off, kw).astype(BF16)
        off += kw
        u_ref[rows, :] = proj(off, d)
        off += d
        for c0 in range(0, d, 2 * LANES):
            z_ref[rows, c0:c0 + 2 * LANES] = _gelu_tanh(proj(off + c0, 2 * LANES)).astype(BF16)
        off += d
        for c0 in range(0, d, 2 * LANES):
            ma_ref[rows, c0:c0 + 2 * LANES] = _sigmoid(proj(off + c0, 2 * LANES)).astype(BF16)
        off += d
        for c0 in range(0, d, 2 * LANES):
            mr_ref[rows, c0:c0 + 2 * LANES] = _sigmoid(proj(off + c0, 2 * LANES)).astype(BF16)


def _inproj_call(l, srcs, modcat, g, w_in_bf, cos_t, sin_t, *, n_ctx, tp):
    b, d = srcs[0].shape[0], srcs[0].shape[2]
    t = sum(s.shape[1] for s in srcs)
    split_src = len(srcs) == 2
    in_w = w_in_bf.shape[2]
    parts = max(c for c in range(1, INPROJ_MAX_ROWS // tp + 1) if (t // tp) % c == 0)
    tt = parts * tp
    nct = n_ctx // tp
    qw = N_HEADS * HEAD_DIM
    kw = N_KV_HEADS * HEAD_DIM
    row = lambda bi, ti: (bi, ti, 0)
    outs = [jax.ShapeDtypeStruct((b, t, qw), BF16), jax.ShapeDtypeStruct((b, t, kw), BF16),
            jax.ShapeDtypeStruct((b, t, kw), BF16), jax.ShapeDtypeStruct((b, t, d), F32),
            jax.ShapeDtypeStruct((b, t, d), BF16), jax.ShapeDtypeStruct((b, t, d), BF16),
            jax.ShapeDtypeStruct((b, t, d), BF16)]
    part_spec = lambda f: pl.BlockSpec((None, tp, d), f)
    if split_src:
        x_specs = ([part_spec(lambda bi, ti, r=r: (bi, jnp.minimum(ti * parts + r, nct - 1), 0)) for r in range(parts)]
                   + [part_spec(lambda bi, ti, r=r: (bi, jnp.maximum(ti * parts + r - nct, 0), 0))
                      for r in range(parts)])
        x_args = [srcs[0]] * parts + [srcs[1]] * parts
    else:
        x_specs = [part_spec(lambda bi, ti, r=r: (bi, ti * parts + r, 0)) for r in range(parts)]
        x_args = [srcs[0]] * parts
    return pl.pallas_call(
        functools.partial(_inproj_body, d=d, n_ctx=n_ctx, parts=parts, split_src=split_src),
        grid=(b, t // tt),
        in_specs=x_specs + [
                  pl.BlockSpec((None, None, 2, N_MOD, d), lambda bi, ti: (l, bi, 0, 0, 0)),
                  pl.BlockSpec((None, 1, d), lambda bi, ti: (l, 0, 0)),
                  pl.BlockSpec((None, d, in_w), lambda bi, ti: (l, 0, 0)),
                  pl.BlockSpec((tt, HEAD_DIM), lambda bi, ti: (ti, 0)),
                  pl.BlockSpec((tt, HEAD_DIM), lambda bi, ti: (ti, 0))],
        out_specs=[pl.BlockSpec((None, tt, s.shape[2]), row) for s in outs],
        out_shape=outs,
        compiler_params=_params(("parallel", "parallel")),
        name="in_proj",
    )(*x_args, modcat, g, w_in_bf, cos_t, sin_t)


def _attn_body(q_ref, k_ref, v_ref, sink_ref, o_ref, *, n_ctx, t_all, blk_off, qb):
    nc = n_ctx // ATT_BLOCK
    band = 3 * ATT_BLOCK
    nt = (((1,), (1,)), ((), ()))
    tn = (((0,), (0,)), ((), ()))
    step = pl.program_id(1)

    def blocks(use_band):
        for j in range(qb):
            n = step * qb + j + blk_off
            qrows = slice(j * ATT_BLOCK, (j + 1) * ATT_BLOCK)
            if use_band:
                start = jnp.clip(n_ctx + (n - nc - 1) * ATT_BLOCK, 0, t_all - band)
                start = pl.multiple_of(start, ATT_BLOCK)
                kpos = start - n_ctx + lax.broadcasted_iota(I32, (band, ATT_BLOCK), 0)
                qpos = (n - nc) * ATT_BLOCK + lax.broadcasted_iota(I32, (band, ATT_BLOCK), 1)
                valid = (kpos >= 0) & (jnp.abs(qpos - kpos) <= ATT_BLOCK) & (n >= nc)
                bias = jnp.where(valid, 0.0, NEG_BIG)
                bias = jnp.concatenate([bias] * GROUP, axis=1)

            for kh in range(N_KV_HEADS):
                lanes = slice(kh * HEAD_DIM, (kh + 1) * HEAD_DIM)
                qs = jnp.concatenate(
                    [q_ref[qrows, (kh * GROUP + g) * HEAD_DIM:(kh * GROUP + g + 1) * HEAD_DIM]
                     for g in range(GROUP)], axis=0)
                kc = k_ref[0:n_ctx, lanes]
                vc = v_ref[0:n_ctx, lanes]
                s_ctx = lax.dot_general(kc, qs, nt, preferred_element_type=F32)
                sink = jnp.concatenate(
                    [sink_ref[kh * GROUP + g:kh * GROUP + g + 1, :] for g in range(GROUP)], axis=1) * LOG2E
                m = jnp.maximum(sink, jnp.max(s_ctx, axis=0, keepdims=True))
                if use_band:
                    kb = k_ref[pl.ds(start, band), lanes]
                    vb = v_ref[pl.ds(start, band), lanes]
                    s_band = lax.dot_general(kb, qs, nt, preferred_element_type=F32) + bias
                    m = jnp.maximum(m, jnp.max(s_band, axis=0, keepdims=True))
                p_ctx = jnp.exp2(s_ctx - m)
                denom = jnp.exp2(sink - m) + jnp.sum(p_ctx, axis=0, keepdims=True)
                acc = lax.dot_general(vc, p_ctx.astype(BF16), tn, preferred_element_type=F32)
                if use_band:
                    p_band = jnp.exp2(s_band - m)
                    denom = denom + jnp.sum(p_band, axis=0, keepdims=True)
                    acc = acc + lax.dot_general(vb, p_band.astype(BF16), tn, preferred_element_type=F32)
                o_t = acc * (1.0 / denom)
                for g in range(GROUP):
                    hh = kh * GROUP + g
                    o_ref[qrows, hh * HEAD_DIM:(hh + 1) * HEAD_DIM] = (
                        o_t[:, g * ATT_BLOCK:(g + 1) * ATT_BLOCK].T.astype(BF16))

    if blk_off == 0 and nc > 0 and nc % qb == 0:
        ctx_step = step < nc // qb

        @pl.when(ctx_step)
        def _():
            blocks(False)

        @pl.when(jnp.logical_not(ctx_step))
        def _():
            blocks(True)
    else:
        blocks(True)


def _attn_call(l, q, k, v, sink_rows, *, n_ctx, with_ctx):
    b, t, qw = q.shape
    kw = k.shape[2]
    blk_off = 0 if with_ctx else n_ctx // ATT_BLOCK
    nblk = t // ATT_BLOCK - blk_off
    qb = ATT_BLOCKS_PER_STEP if nblk % ATT_BLOCKS_PER_STEP == 0 and blk_off % ATT_BLOCKS_PER_STEP == 0 else 1
    rows = qb * ATT_BLOCK
    return pl.pallas_call(
        functools.partial(_attn_body, n_ctx=n_ctx, t_all=t, blk_off=blk_off, qb=qb),
        grid=(b, nblk // qb),
        in_specs=[pl.BlockSpec((None, rows, qw), lambda bi, ni: (bi, ni + blk_off // qb, 0)),
                  pl.BlockSpec((None, t, kw), lambda bi, ni: (bi, 0, 0)),
                  pl.BlockSpec((None, t, kw), lambda bi, ni: (bi, 0, 0)),
                  pl.BlockSpec((None, N_HEADS, LANES), lambda bi, ni: (l, 0, 0))],
        out_specs=pl.BlockSpec((None, rows, qw), lambda bi, ni: (bi, ni, 0)),
        out_shape=jax.ShapeDtypeStruct((b, nblk * ATT_BLOCK, qw), BF16),
        compiler_params=_params(("parallel", "arbitrary")),
        name="win_attn",
    )(q, k, v, sink_rows)


def _lru_body(u_ref, cw_ref, cb_ref, w_ref, lam_ref, y_ref,
              p_f, a_b, b_b, h_f, h_b, y32, *, n_ctx, n_lat):
    t_all = n_ctx + n_lat
    nseg = SCAN_SEGMENTS
    seg = t_all // nseg

    lam = lam_ref[...]
    nl = -lam
    sp = jnp.maximum(nl, 0.0) + jnp.log(1.0 + jnp.exp(-jnp.abs(nl)))
    c2 = sp * (-0.5 * LRU_C * LOG2E)
    cw = cw_ref[...]
    cb = cb_ref[...]
    wcat = w_ref[...]
    lane = lax.broadcasted_iota(I32, (SCAN_SEGMENTS, LANES), 1)
    bias_cols = jnp.where(lane < 2, 1.0, 0.0).astype(BF16)

    def rows(i):
        return pl.ds(i, nseg, stride=seg)

    seg_id = lax.broadcasted_iota(I32, (nseg, LANES), 0)
    ones = jnp.ones((nseg, LANES), F32)
    zeros = jnp.zeros((nseg, LANES), F32)

    def same_part(r, q):
        return (r < n_ctx) == (q < n_ctx)

    def tap(i, off):
        j = i + off
        if 0 <= j < seg:
            v = u_ref[rows(j), :]
        elif j < 0:
            v = pltpu.roll(u_ref[rows(j + seg), :], 1, 0)
        else:
            v = pltpu.roll(u_ref[rows(j - seg), :], nseg - 1, 0)
        for s in range(nseg):
            r = i + seg * s
            if not (0 <= r + off < t_all and same_part(r, r + off)):
                v = jnp.where(seg_id == s, 0.0, v)
        return v

    brk_s, brk_i = (n_ctx - 1) // seg, (n_ctx - 1) % seg

    hf_fin, pf_tot = zeros, ones
    for i in range(seg):
        uc = cb
        for kk in range(CONV_W):
            uc = uc + cw[kk:kk + 1, :] * tap(i, kk - CONV_LEFT)
        lhs = jnp.concatenate([uc.astype(BF16), bias_cols], axis=1)
        th = jnp.tanh(jnp.dot(lhs, wcat, preferred_element_type=F32))
        uh = 0.5 * uc
        ab_dir = []
        for dr in range(2):
            t_r = th[:, (2 * dr) * LANES:(2 * dr + 1) * LANES]
            t_i = th[:, (2 * dr + 1) * LANES:(2 * dr + 2) * LANES]
            cc = c2[dr:dr + 1, :]
            a = jnp.exp2(t_r * cc + cc)
            om = 1.0 - a * a
            root = jnp.where(om > 0.0, om * lax.rsqrt(om), 0.0)
            ab_dir.append((a, root * (t_i * uh + uh)))
        af, bf = ab_dir[0]
        hf_fin = af * hf_fin + bf
        pf_tot = af * pf_tot
        p_f[rows(i), :] = pf_tot
        h_f[rows(i), :] = hf_fin
        ab, bb = ab_dir[1]
        if i == brk_i:
            ab = jnp.where(seg_id == brk_s, 0.0, ab)
        a_b[rows(i), :] = ab
        b_b[rows(i), :] = bb

    hin = jnp.zeros((1, LANES), F32)
    hin_f = []
    for s in range(nseg):
        hin_f.append(hin)
        hin = hf_fin[s:s + 1, :] + pf_tot[s:s + 1, :] * hin
    hin_f = jnp.concatenate(hin_f, axis=0)

    hb_fin, pb_tot = zeros, ones
    for i in range(seg):
        j = seg - 1 - i
        ab = a_b[rows(j), :]
        hb_fin = ab * hb_fin + b_b[rows(j), :]
        pb_tot = ab * pb_tot
        a_b[rows(j), :] = pb_tot
        h_b[rows(j), :] = hb_fin
        h_f[rows(i), :] = h_f[rows(i), :] + p_f[rows(i), :] * hin_f

    order = [(brk_s - k) % nseg for k in range(nseg)]
    hin_b = [None] * nseg
    hin = jnp.zeros((1, LANES), F32)
    for s in order:
        hin_b[s] = hin
        hin = hb_fin[s:s + 1, :] + pb_tot[s:s + 1, :] * hin
    hin_b[brk_s] = hin
    hin_b = jnp.concatenate(hin_b, axis=0)

    for j in range(seg):
        y32[rows(j), :] = h_f[rows(j), :] + (h_b[rows(j), :] + a_b[rows(j), :] * hin_b)
    y_ref[...] = y32[...].astype(BF16)


def _lru_call(l, u, conv_w, conv_b, wcat, lam, *, n_ctx):
    b, t, w = u.shape
    n_lat = t - n_ctx
    assert t % SCAN_SEGMENTS == 0 and w == LRU_BLOCKS * LANES and n_ctx > 0
    scr = [pltpu.VMEM((t, LANES), F32) for _ in range(6)]
    return pl.pallas_call(
        functools.partial(_lru_body, n_ctx=n_ctx, n_lat=n_lat),
        grid=(b, LRU_BLOCKS),
        in_specs=[pl.BlockSpec((None, t, LANES), lambda bi, ci: (bi, 0, ci)),
                  pl.BlockSpec((None, CONV_W, LANES), lambda bi, ci: (l, 0, ci)),
                  pl.BlockSpec((None, 1, LANES), lambda bi, ci: (l, 0, ci)),
                  pl.BlockSpec((None, None, 2 * LANES, 4 * LANES), lambda bi, ci: (l, ci, 0, 0)),
                  pl.BlockSpec((None, 2, LANES), lambda bi, ci: (l, 0, ci))],
        out_specs=pl.BlockSpec((None, t, LANES), lambda bi, ci: (bi, 0, ci)),
        out_shape=jax.ShapeDtypeStruct((b, t, w), BF16),
        scratch_shapes=scr,
        compiler_params=_params(("parallel", "parallel")),
        name="rg_lru",
    )(u, conv_w, conv_b, wcat, lam)


def _merge_body(*refs, n_ctx, split_src):
    if split_src:
        xctx_ref, x_ref = refs[0], refs[1]
        refs = refs[1:]
    (x_ref, att_ref, y_ref, gz_ref, sa_ref, sr_ref, mod_ref, g_ref,
     wab_ref, wrb_ref, wo_ref, wrh_ref, wrl_ref, xo_ref, h2_ref, aff_ref, mix_s) = refs
    tt, d = x_ref.shape
    from_ctx = pl.program_id(1) * tt < n_ctx

    def x_in(cols):
        if split_src:
            return jnp.where(from_ctx, xctx_ref[:, cols], x_ref[:, cols])
        return x_ref[:, cols]

    ch = 2 * LANES
    n_exp = aff_ref.shape[0]
    att = att_ref[...]
    rec_in = y_ref[...] * gz_ref[...]
    for c0 in range(0, d, ch):
        cols = slice(c0, c0 + ch)
        att_d = jnp.dot(att, wab_ref[:, cols], preferred_element_type=F32)
        rec_d = jnp.dot(rec_in, wrb_ref[:, cols], preferred_element_type=F32)
        mix = sa_ref[:, cols].astype(F32) * att_d + sr_ref[:, cols].astype(F32) * rec_d
        mix_s[:, cols] = mix.astype(BF16)
    mix = mix_s[...]
    part = jnp.zeros((tt, LANES), F32)
    for c0 in range(0, d, ch):
        cols = slice(c0, c0 + ch)
        out = jnp.dot(mix, wo_ref[:, cols], preferred_element_type=F32)
        xn = x_in(cols) + mod_ref[2:3, cols] * out
        xo_ref[:, cols] = xn
        sq = xn * xn
        part = part + sq[:, 0:LANES] + sq[:, LANES:ch]
    rs = lax.rsqrt(jnp.sum(part, axis=-1, keepdims=True) * (1.0 / d) + EPS)
    logits = jnp.zeros((tt, LANES), F32)
    for c0 in range(0, d, ch):
        cols = slice(c0, c0 + ch)
        h2 = xo_ref[:, cols] * rs * g_ref[:, cols] * (1.0 + mod_ref[4:5, cols]) + mod_ref[3:4, cols]
        hi = h2.astype(BF16)
        h2_ref[:, cols] = hi
        lo = (h2 - hi.astype(F32)).astype(BF16)
        logits = (logits + jnp.dot(hi, wrh_ref[cols, :], preferred_element_type=F32)
                  + jnp.dot(lo, wrh_ref[cols, :], preferred_element_type=F32)
                  + jnp.dot(hi, wrl_ref[cols, :], preferred_element_type=F32))
    lt = logits.T[0:n_exp, :]
    mx = jnp.max(lt, axis=0, keepdims=True)
    ex = jnp.exp(lt - mx)
    aff_ref[...] = ex / jnp.sum(ex, axis=0, keepdims=True)


def _merge_call(l, srcs, att, y, gz, sa, sr, modcat, g, wab, wrb, wo, wr_hi, wr_lo, *, n_ctx, tt, with_ctx):
    b, d = srcs[0].shape[0], srcs[0].shape[2]
    t = sum(s.shape[1] for s in srcs)
    split_src = len(srcs) == 2
    assert with_ctx or not split_src
    nct = n_ctx // tt
    off = 0 if with_ctx else nct
    t_out = t - off * tt
    row = lambda bi, ti: (bi, ti + off, 0)
    orow = lambda bi, ti: (bi, ti, 0)
    lay = lambda bi, ti: (l, 0, 0)
    if split_src:
        x_specs = [pl.BlockSpec((None, tt, d), lambda bi, ti: (bi, jnp.minimum(ti, nct - 1), 0)),
                   pl.BlockSpec((None, tt, d), lambda bi, ti: (bi, jnp.maximum(ti - nct, 0), 0))]
    else:
        x_specs = [pl.BlockSpec((None, tt, d), row)]
    return pl.pallas_call(
        functools.partial(_merge_body, n_ctx=n_ctx, split_src=split_src),
        grid=(b, t // tt - off),
        in_specs=x_specs + [pl.BlockSpec((None, tt, d), orow)]
        + [pl.BlockSpec((None, tt, d), row)] * 4 + [
            pl.BlockSpec((None, None, None, N_MOD, d),
                         lambda bi, ti: (l, bi, jnp.where(ti + off >= nct, 1, 0), 0, 0)),
            pl.BlockSpec((None, 1, d), lay),
            pl.BlockSpec((None,) + wab.shape[1:], lay), pl.BlockSpec((None,) + wrb.shape[1:], lay),
            pl.BlockSpec((None,) + wo.shape[1:], lay),
            pl.BlockSpec((None, d, LANES), lay), pl.BlockSpec((None, d, LANES), lay)],
        out_specs=[pl.BlockSpec((None, tt, d), orow), pl.BlockSpec((None, tt, d), orow),
                   pl.BlockSpec((None, N_EXPERTS, tt), lambda bi, ti: (bi, 0, ti))],
        out_shape=[jax.ShapeDtypeStruct((b, t_out, d), F32), jax.ShapeDtypeStruct((b, t_out, d), BF16),
                   jax.ShapeDtypeStruct((b, N_EXPERTS, t_out), F32)],
        scratch_shapes=[pltpu.VMEM((tt, d), BF16)],
        compiler_params=_params(("parallel", "parallel")),
        name="merge_router",
    )(*srcs, att, y, gz, sa, sr, modcat, g, wab, wrb, wo, wr_hi, wr_lo)


def _prefix_excl(mask, tri):
    rows, length = mask.shape
    run = jnp.zeros((rows, 1), F32)
    parts = []
    for kb in range(length // LANES):
        blk = mask[:, kb * LANES:(kb + 1) * LANES]
        loc = jnp.dot(blk.astype(BF16), tri, preferred_element_type=F32)
        parts.append(loc + run)
        run = run + jnp.sum(blk, axis=1, keepdims=True)
    return jnp.concatenate(parts, axis=1)


def _count_ge(aff, cand):
    return jnp.sum(jnp.where(aff >= cand, 1.0, 0.0), axis=1, keepdims=True)


def _select_topcap(segments, tri):
    n = len(segments)
    affs = [s[0] for s in segments]
    caps = [float(s[1]) for s in segments]
    rows = affs[0].shape[0]
    base = jnp.full((rows, 1), MIN_NORMAL, F32)
    has = [_count_ge(affs[i], base) >= caps[i] for i in range(n)]

    def best(i, cur, cands):
        for c in cands:
            cur = jnp.where(_count_ge(affs[i], c) >= caps[i], c, cur)
        return cur

    lo = [base] * n
    for j in range(6, -1, -2):
        fa = float(2.0 ** (2 ** j))
        fb = float(2.0 ** (2 ** (j - 1))) if j > 0 else None
        for i in range(n):
            mults = [fa] if fb is None else [fb, fa, fa * fb]
            lo[i] = best(i, lo[i], [lo[i] * m for m in mults])
    lo = [jnp.where(has[i], lo[i], 0.0) for i in range(n)]
    cur = list(lo)
    for j in range(1, MANTISSA_STEPS + 1, 2):
        q = float(2.0 ** -(j + 1))
        for i in range(n):
            cur[i] = best(i, cur[i], [cur[i] + lo[i] * (k * q) for k in (1, 2, 3)])
    codes = []
    for i in range(n):
        aff = affs[i]
        nxt = jnp.where(has[i], cur[i] + lo[i] * float(2.0 ** -23), base)
        ge = jnp.where(aff >= cur[i], 1.0, 0.0)
        tie = ge * jnp.where(aff < nxt, 1.0, 0.0)
        excess = jnp.sum(ge, axis=1, keepdims=True) - caps[i]
        after = jnp.sum(tie, axis=1, keepdims=True) - (_prefix_excl(tie, tri) + tie)
        sel = ge - tie * jnp.where(after < excess, 1.0, 0.0)
        pos = _prefix_excl(sel, tri)
        codes.append(jnp.where(sel > 0.5, pos, -1.0))
    return codes


def _route_body(aff_ref, code_ref, affrow_ref, bnd_ref, *, n_ctx, n_lat, tt):
    aff = aff_ref[...]
    e = aff.shape[0]
    r = lax.broadcasted_iota(I32, (LANES, LANES), 0)
    c = lax.broadcasted_iota(I32, (LANES, LANES), 1)
    tri = (r < c).astype(BF16)
    cap_lat = EC_FACTOR * n_lat // N_EXPERTS
    if n_ctx:
        cap_ctx = EC_FACTOR * n_ctx // N_EXPERTS
        code_ctx, code = _select_topcap([(aff[:, 0:n_ctx], cap_ctx), (aff[:, n_ctx:n_ctx + n_lat], cap_lat)], tri)
        code = jnp.concatenate([code_ctx, jnp.where(code >= 0.0, code + cap_ctx, -1.0)], axis=1)
    else:
        (code,) = _select_topcap([(aff, cap_lat)], tri)
    taken = jnp.where(code >= 0.0, 1.0, 0.0)
    run = jnp.zeros((e, 1), F32)
    bnds = [run]
    for t0 in range(0, n_ctx + n_lat, tt):
        run = run + jnp.sum(taken[:, t0:t0 + tt], axis=1, keepdims=True)
        bnds.append(run)
    bnd_ref[...] = jnp.concatenate(bnds, axis=1).astype(I32)
    code = code.astype(I32)
    for ei in range(e):
        code_ref[ei] = code[ei:ei + 1, :]
        affrow_ref[ei] = aff[ei:ei + 1, :]


def _route_call(aff, *, n_ctx, tt):
    b, e, t = aff.shape
    spec4 = pl.BlockSpec((None, e, 1, t), lambda bi: (bi, 0, 0, 0))
    nb = t // tt + 1
    return pl.pallas_call(
        functools.partial(_route_body, n_ctx=n_ctx, n_lat=t - n_ctx, tt=tt),
        grid=(b,),
        in_specs=[pl.BlockSpec((None, e, t), lambda bi: (bi, 0, 0))],
        out_specs=[spec4, spec4, pl.BlockSpec((None, e, nb), lambda bi: (bi, 0, 0))],
        out_shape=[jax.ShapeDtypeStruct((b, e, 1, t), I32), jax.ShapeDtypeStruct((b, e, 1, t), F32),
                   jax.ShapeDtypeStruct((b, e, nb), I32)],
        compiler_params=_params(("parallel",)),
        name="route_select",
    )(aff)


def _moe_body(code_ref, aff_ref, h_ref, wg32_ref, wu32_ref, wd32_ref, ye_ref, wg_s, wu_s, wd_s,
              *, n_ctx, n_ctx_slots):
    p = pl.program_id(0)
    bi = pl.program_id(1)
    fill = p % 2
    use = 1 - fill

    def store_weight_slice():
        rd = wg32_ref.shape[0]
        rf = wd32_ref.shape[0]
        r0 = pl.multiple_of(bi * rd, rd)
        f0 = pl.multiple_of(bi * rf, rf)
        wg_s[fill, pl.ds(r0, rd), :] = wg32_ref[...].astype(BF16)
        wu_s[fill, pl.ds(r0, rd), :] = wu32_ref[...].astype(BF16)
        wd_s[fill, pl.ds(f0, rf), :] = wd32_ref[...].astype(BF16)

    @pl.when(p == 0)
    def _():
        store_weight_slice()
        ye_ref[...] = jnp.zeros(ye_ref.shape, ye_ref.dtype)

    @pl.when(p > 0)
    def _():
        store_weight_slice()
        n_slots = ye_ref.shape[0]
        t = code_ref.shape[1]
        xs, gs = [], []
        for (s0, s1, t0, t1) in ((0, n_ctx_slots, 0, n_ctx), (n_ctx_slots, n_slots, n_ctx, t)):
            if s1 == s0:
                continue
            hit = code_ref[:, t0:t1] == s0 + lax.broadcasted_iota(I32, (s1 - s0, t1 - t0), 0)
            sel = jnp.where(hit, 1.0, 0.0).astype(BF16)
            xs.append(jnp.dot(sel, h_ref[t0:t1, :], preferred_element_type=F32).astype(BF16))
            gs.append(jnp.sum(jnp.where(hit, aff_ref[:, t0:t1], 0.0), axis=1, keepdims=True))
        xe = jnp.concatenate(xs, axis=0)
        gcol = jnp.concatenate(gs, axis=0)
        gate = jnp.dot(xe, wg_s[use], preferred_element_type=F32)
        up = jnp.dot(xe, wu_s[use], preferred_element_type=F32)
        hid = (gate * _sigmoid(gate) * up).astype(BF16)
        ye = jnp.dot(hid, wd_s[use], preferred_element_type=F32) * gcol
        ye_ref[...] = ye.astype(BF16)


def _moe_call(l, code, aff4, h2, wg, wu, wd, *, n_slots, n_ctx):
    b, t, d = h2.shape
    _, e, _, f = wg.shape
    assert d % (b * 16) == 0 and f % (b * 16) == 0
    rd, rf = d // b, f // b
    n_ctx_slots = EC_FACTOR * n_ctx // N_EXPERTS
    ex = lambda pi: jnp.maximum(pi - 1, 0)
    wmap = lambda pi, bi: (l, jnp.minimum(pi, e - 1), jnp.where(pi < e, bi, b - 1), 0)
    return pl.pallas_call(
        functools.partial(_moe_body, n_ctx=n_ctx, n_ctx_slots=n_ctx_slots),
        grid=(e + 1, b),
        in_specs=[pl.BlockSpec((None, None, 1, t), lambda pi, bi: (bi, ex(pi), 0, 0)),
                  pl.BlockSpec((None, None, 1, t), lambda pi, bi: (bi, ex(pi), 0, 0)),
                  pl.BlockSpec((None, t, d), lambda pi, bi: (jnp.where(pi == 0, 0, bi), 0, 0)),
                  pl.BlockSpec((None, None, rd, f), wmap),
                  pl.BlockSpec((None, None, rd, f), wmap),
                  pl.BlockSpec((None, None, rf, d), wmap)],
        out_specs=pl.BlockSpec((None, None, n_slots, d), lambda pi, bi: (bi, jnp.where(pi == 0, e, pi - 1), 0, 0)),
        out_shape=jax.ShapeDtypeStruct((b, e + 1, n_slots, d), BF16),
        scratch_shapes=[pltpu.VMEM((2, d, f), BF16), pltpu.VMEM((2, d, f), BF16), pltpu.VMEM((2, f, d), BF16)],
        compiler_params=_params(("arbitrary", "arbitrary")),
        name="moe_ffn",
    )(code, aff4, h2, wg, wu, wd)


def _combine_body(bnd_ref, x_ref, code_ref, ye_ref, mod_ref, gf_ref, o_ref, *, n_slots, final, win):
    bi = pl.program_id(0)
    ti = pl.program_id(1)
    nb = pl.num_programs(1) + 1
    e = ye_ref.shape[0]
    tt, d = x_ref.shape
    tn = (((0,), (0,)), ((), ()))
    starts = []
    fits = None
    for ei in range(e):
        at = (bi * e + ei) * nb + ti
        lo = bnd_ref[at]
        hi = bnd_ref[at + 1]
        st = jnp.minimum(lax.shift_left(lax.shift_right_logical(lo, 4), 4), n_slots - win)
        starts.append(pl.multiple_of(st, 16))
        ok = hi - st <= win
        fits = ok if fits is None else jnp.logical_and(fits, ok)

    def finish(acc):
        xn = x_ref[...] + mod_ref[5:6, :] * acc
        if final:
            xn = _rmsnorm(xn, gf_ref[...])
        o_ref[...] = xn

    @pl.when(fits)
    def _():
        slot = lax.broadcasted_iota(I32, (win, tt), 0)
        sel = jnp.concatenate(
            [jnp.where(code_ref[ei] - starts[ei] == slot, 1.0, 0.0).astype(BF16) for ei in range(e)], axis=0)
        ye = jnp.concatenate([ye_ref[ei, pl.ds(starts[ei], win), :] for ei in range(e)], axis=0)
        finish(lax.dot_general(sel, ye, tn, preferred_element_type=F32))

    @pl.when(jnp.logical_not(fits))
    def _():
        slot = lax.broadcasted_iota(I32, (n_slots, tt), 0)
        sel = jnp.concatenate(
            [jnp.where(code_ref[ei] == slot, 1.0, 0.0).astype(BF16) for ei in range(e)], axis=0)
        ye = ye_ref[...].reshape(e * n_slots, d)
        finish(lax.dot_general(sel, ye, tn, preferred_element_type=F32))


def _combine_call(l, xc, code, bounds, ye, modcat, gfinal, *, n_ctx, tt, final):
    b, t, d = xc.shape
    e, n_slots = N_EXPERTS, ye.shape[2]
    nct = n_ctx // tt
    win = min(COMBINE_WINDOW, n_slots)
    grid_spec = pltpu.PrefetchScalarGridSpec(
        num_scalar_prefetch=1,
        grid=(b, t // tt),
        in_specs=[pl.BlockSpec((None, tt, d), lambda bi, ti, bnd: (bi, ti, 0)),
                  pl.BlockSpec((None, e, 1, tt), lambda bi, ti, bnd: (bi, 0, 0, ti)),
                  pl.BlockSpec((None, e, n_slots, d), lambda bi, ti, bnd: (bi, 0, 0, 0)),
                  pl.BlockSpec((None, None, None, N_MOD, d),
                               lambda bi, ti, bnd: (l, bi, jnp.where(ti >= nct, 1, 0), 0, 0)),
                  pl.BlockSpec((1, d), lambda bi, ti, bnd: (0, 0))],
        out_specs=pl.BlockSpec((None, tt, d), lambda bi, ti, bnd: (bi, ti, 0)))
    return pl.pallas_call(
        functools.partial(_combine_body, n_slots=n_slots, final=final, win=win),
        grid_spec=grid_spec,
        out_shape=jax.ShapeDtypeStruct((b, t, d), F32),
        compiler_params=_params(("parallel", "arbitrary")),
        name="moe_combine",
    )(bounds.reshape(-1), xc, code, ye, modcat, gfinal)


def _rope_tables(n_ctx, n_lat):
    quarter = HEAD_DIM // 4
    t = jnp.arange(n_lat, dtype=jnp.int32)
    rows = (t // GRID_W).astype(F32)
    cols = (t % GRID_W).astype(F32)
    freqs = ROPE_THETA ** (-jnp.arange(quarter, dtype=F32) / quarter)
    ang_r = rows[:, None] * freqs[None, :]
    ang_c = cols[:, None] * freqs[None, :]
    cos = jnp.concatenate([jnp.cos(ang_r), jnp.cos(ang_r), jnp.cos(ang_c), jnp.cos(ang_c)], axis=1)
    sin = jnp.concatenate([-jnp.sin(ang_r), jnp.sin(ang_r), -jnp.sin(ang_c), jnp.sin(ang_c)], axis=1)
    cos = jnp.concatenate([jnp.ones((n_ctx, HEAD_DIM), F32), cos], axis=0)
    sin = jnp.concatenate([jnp.zeros((n_ctx, HEAD_DIM), F32), sin], axis=0)
    return cos, sin


def kernel(x, c, ctx, c_ctx, ada_w, ada_b, norm_mix_g, w_in, attn_sink, conv_w, conv_b, lru_wa, lru_ba, lru_wx,
           lru_bx, lru_lambda, w_attn_br, w_rec_br, w_out, norm_ffn_g, w_router, w_gate, w_up, w_down,
           final_norm_g):
    b, n_lat, d = x.shape
    n_ctx = ctx.shape[1]
    depth = ada_w.shape[0]
    tt = math.gcd(math.gcd(n_ctx, n_lat), 256)

    pad_rows = (-(b + 1)) % SUBLANES
    cc = jnp.concatenate([c, c_ctx[None, :], jnp.zeros((pad_rows, d), F32)], axis=0)
    mod = _ada_call(cc, ada_w, ada_b)
    mod_lat = mod[:, :b].reshape(depth, b, 1, N_MOD, d)
    mod_ctx = jnp.broadcast_to(mod[:, b].reshape(depth, 1, 1, N_MOD, d), (depth, b, 1, N_MOD, d))
    modcat = jnp.concatenate([mod_ctx, mod_lat], axis=2)

    cos_t, sin_t = _rope_tables(n_ctx, n_lat)
    srcs = (ctx, x) if depth > 1 else (jnp.concatenate([ctx, x], axis=1),)

    w_in_bf = w_in.astype(BF16)
    wab_bf = w_attn_br.astype(BF16)
    wrb_bf = w_rec_br.astype(BF16)
    wo_bf = w_out.astype(BF16)
    wcat = (0.5 * jnp.concatenate([lru_wa[:, 0], lru_wx[:, 0], lru_wa[:, 1], lru_wx[:, 1]], axis=-1)).astype(BF16)
    bcat = 0.5 * jnp.stack([lru_ba[:, 0], lru_bx[:, 0], lru_ba[:, 1], lru_bx[:, 1]], axis=1)
    bcat = bcat.reshape(depth, 4, LRU_BLOCKS, LANES).transpose(0, 2, 1, 3).reshape(depth, LRU_BLOCKS, 1, 4 * LANES)
    b_hi = bcat.astype(BF16)
    b_lo = (bcat - b_hi.astype(F32)).astype(BF16)
    wcat = jnp.concatenate(
        [wcat, b_hi, b_lo, jnp.zeros((depth, LRU_BLOCKS, LANES - 2, 4 * LANES), BF16)], axis=2)
    sink_rows = jnp.broadcast_to(attn_sink[:, :, None], (depth, N_HEADS, LANES))
    wr_pad = jnp.pad(w_router, ((0, 0), (0, 0), (0, LANES - w_router.shape[2])))
    wr_hi = wr_pad.astype(BF16)
    wr_lo = (wr_pad - wr_hi.astype(F32)).astype(BF16)
    g_mix = norm_mix_g[:, None, :]
    g_ffn = norm_ffn_g[:, None, :]
    conv_b3 = conv_b[:, None, :]

    cap_lat = EC_FACTOR * n_lat // N_EXPERTS
    cap_ctx = EC_FACTOR * n_ctx // N_EXPERTS
    for l in range(depth):
        last = l == depth - 1
        with_ctx = not last
        q, k, v, u, gz, sa, sr = _inproj_call(l, srcs, modcat, g_mix, w_in_bf, cos_t, sin_t, n_ctx=n_ctx, tp=tt)
        att = _attn_call(l, q, k, v, sink_rows, n_ctx=n_ctx, with_ctx=with_ctx)
        y = _lru_call(l, u, conv_w, conv_b3, wcat, lru_lambda, n_ctx=n_ctx)
        xc, h2, aff = _merge_call(l, srcs, att, y, gz, sa, sr, modcat, g_ffn, wab_bf, wrb_bf, wo_bf, wr_hi, wr_lo,
                                  n_ctx=n_ctx, tt=tt, with_ctx=with_ctx)
        ctx_rows = n_ctx if with_ctx else 0
        code, aff_rows, bounds = _route_call(aff, n_ctx=ctx_rows, tt=tt)
        n_slots = cap_lat + (cap_ctx if with_ctx else 0)
        ye = _moe_call(l, code, aff_rows, h2, w_gate, w_up, w_down, n_slots=n_slots, n_ctx=ctx_rows)
        xc = _combine_call(l, xc, code, bounds, ye, modcat, final_norm_g[None, :], n_ctx=ctx_rows, tt=tt, final=last)
        srcs = (xc,)
    return xc
```
